```python
import jax, jax.numpy as jnp
from jax import lax
import numpy as np

D_MODEL = 1024
BATCH = 8
SEQ = 4096
DEPTH = 4

N_A = DEPTH // 2
N_B = DEPTH - N_A
PLE_DIM = 256
NORM_EPS = 1e-6
RW_HEAD = 64
RW_WIDTH = D_MODEL
RW_HEADS = RW_WIDTH // RW_HEAD
DECAY_LORA = 64
AAA_LORA = 64
MV_LORA = 32
GN_EPS = 64e-5
ATT_HEAD = 128
KV_HEADS = 8
ATT_WIDTH = KV_HEADS * ATT_HEAD
DILATION_GROUPS = ((128, 1), (512, 4), (2048, 16))
N_GROUPS = len(DILATION_GROUPS)
Q_WIDTH = N_GROUPS * ATT_WIDTH
BLOCK = 128

kernel_name = "yoco_rwkv7_dilated_alibi_hybrid"


def rmsnorm(x, g):
    xf = x.astype(jnp.float32)
    y = xf * lax.rsqrt(jnp.mean(xf * xf, axis=-1, keepdims=True) + NORM_EPS)
    return (y * g.astype(jnp.float32)).astype(x.dtype)


def token_shift(x):
    return jnp.pad(x, ((0, 0), (1, 0), (0, 0)))[:, :-1]


def wkv7(r, decay, k, v, kk, a):
    B, S, H, N = r.shape

    def step(state, inp):
        r_t, w_t, k_t, v_t, kk_t, a_t = inp
        sa = jnp.einsum('bhij,bhj->bhi', state, -kk_t)
        state = (state * w_t[:, :, None, :]
                 + sa[..., None] * (kk_t * a_t)[:, :, None, :]
                 + v_t[..., None] * k_t[:, :, None, :])
        return state, jnp.einsum('bhij,bhj->bhi', state, r_t)

    seq = tuple(jnp.moveaxis(t, 1, 0) for t in (r, decay, k, v, kk, a))
    s0 = jnp.zeros((B, H, N, N), jnp.float32)
    _, y = lax.scan(step, s0, seq)
    return jnp.moveaxis(y, 0, 1)


def rwkv7_time_mix(x, v_first, mu, w_rkvg, w0, w1, w2, a0, a1, a2, vres,
                   k_k, k_a, r_k, gn_g, gn_b, w_o):
    f32 = jnp.float32
    B, S, _ = x.shape
    xx = token_shift(x) - x
    xm = x[None] + xx[None] * mu[:, None, None, :]
    r, k, v, gate = jnp.einsum('nbsd,nde->nbse', xm[:4], w_rkvg)
    w_log = -jax.nn.softplus(-(w0 + jnp.tanh(xm[4] @ w1) @ w2).astype(f32)) - 0.5
    decay = jnp.exp(-jnp.exp(w_log))
    a = jax.nn.sigmoid((a0 + (xm[5] @ a1) @ a2).astype(f32))
    if vres is not None:
        v0, v1, v2 = vres
        v = v + (v_first - v) * jax.nn.sigmoid(v0 + (xm[2] @ v1) @ v2)

    def heads(t):
        return t.astype(f32).reshape(B, S, RW_HEADS, RW_HEAD)

    kk = heads(k * k_k)
    kk = kk / jnp.maximum(jnp.sqrt(jnp.sum(kk * kk, axis=-1, keepdims=True)), 1e-12)
    k = k * (1.0 + (a - 1.0) * k_a)
    rh, kh, vh = heads(r), heads(k), heads(v)
    y = wkv7(rh, heads(decay), kh, vh, kk, heads(a))
    mean = jnp.mean(y, axis=-1, keepdims=True)
    yc = y - mean
    y = yc * lax.rsqrt(jnp.mean(yc * yc, axis=-1, keepdims=True) + GN_EPS)
    y = y.reshape(B, S, RW_WIDTH) * gn_g + gn_b
    y = y + (jnp.sum(rh * kh * r_k, axis=-1, keepdims=True) * vh).reshape(B, S, RW_WIDTH)
    out = (y * jax.nn.silu(gate.astype(f32))) @ w_o
    return out.astype(x.dtype), v


def to_strided_blocks(t, d):
    B, S, H, E = t.shape
    n = S // d
    nb = -(-n // BLOCK)
    t = t.reshape(B, n, d, H, E).transpose(0, 2, 1, 3, 4)
    t = jnp.pad(t, ((0, 0), (0, 0), (0, nb * BLOCK - n), (0, 0), (0, 0)))
    return t.reshape(B, d, nb, BLOCK, H, E)


def from_strided_blocks(t, S):
    B, d, nb, L = t.shape[:4]
    rest = t.shape[4:]
    n = S // d
    t = t.reshape((B, d, nb * L) + rest)[:, :, :n]
    t = jnp.moveaxis(t, 1, 2)
    return t.reshape((B, S) + rest)


def with_prev_block(t):
    prev = jnp.pad(t, ((0, 0), (0, 0), (1, 0), (0, 0), (0, 0), (0, 0)))[:, :, :-1]
    return jnp.concatenate([prev, t], axis=3)


def alibi_slopes(d):
    h = np.arange(1, KV_HEADS + 1, dtype=np.float32)
    return jnp.asarray((2.0 ** (-8.0 * h / KV_HEADS)) / d, dtype=jnp.float32)


def shared_kv_windows(h, kv_ln_g, w_kv):
    B, S, _ = h.shape
    kv = rmsnorm(h, kv_ln_g) @ w_kv
    k = kv[..., :ATT_WIDTH].reshape(B, S, KV_HEADS, ATT_HEAD)
    v = kv[..., ATT_WIDTH:].reshape(B, S, KV_HEADS, ATT_HEAD)
    return tuple((with_prev_block(to_strided_blocks(k, d)), with_prev_block(to_strided_blocks(v, d)))
                 for (_, d) in DILATION_GROUPS)


def dilated_group_attention(q, kw, vw, d, span, slopes):
    S = q.shape[1]
    qb = to_strided_blocks(q, d)
    nb = qb.shape[2]
    s = jnp.einsum('brcihe,brcjhe->brchij', qb, kw).astype(jnp.float32) * (ATT_HEAD ** -0.5)
    i = jnp.arange(BLOCK)[:, None]
    j = jnp.arange(2 * BLOCK)[None, :]
    delta = BLOCK + i - j
    c = jnp.arange(nb)[:, None, None]
    valid = (delta >= 0) & (delta <= span) & (c * BLOCK + j - BLOCK >= 0)
    bias = -slopes[:, None, None] * (delta * d).astype(jnp.float32)[None]
    s = jnp.where(valid[:, None], s + bias, -jnp.inf)
    m = jnp.max(s, axis=-1, keepdims=True)
    pexp = jnp.exp(s - m)
    l = jnp.sum(pexp, axis=-1, keepdims=True)
    o = jnp.einsum('brchij,brcjhe->brcihe', pexp / l, vw)
    lse = jnp.moveaxis((m + jnp.log(l))[..., 0], -1, -2)
    return from_strided_blocks(o, S), from_strided_blocks(lse, S)


def dilated_attention_mix(x, kv_windows, w_in, w_o):
    B, S, _ = x.shape
    proj = x @ w_in
    q = proj[..., :Q_WIDTH].reshape(B, S, N_GROUPS, KV_HEADS, ATT_HEAD)
    gate = proj[..., Q_WIDTH:]
    outs, lses = [], []
    for g, (window, d) in enumerate(DILATION_GROUPS):
        kw, vw = kv_windows[g]
        o, lse = dilated_group_attention(q[:, :, g], kw, vw, d, window // d, alibi_slopes(d))
        outs.append(o)
        lses.append(lse)
    alpha = jax.nn.softmax(jnp.stack(lses), axis=0)
    o = jnp.einsum('gbsh,gbshe->bshe', alpha, jnp.stack(outs)).reshape(B, S, ATT_WIDTH)
    return ((o * jax.nn.silu(gate.astype(jnp.float32))) @ w_o).astype(x.dtype)


def per_layer_embedding(h, p_i, w_p, gate_ln_g, w_gate):
    gate = jax.nn.sigmoid((rmsnorm(h, gate_ln_g) @ w_gate).astype(jnp.float32))
    return ((p_i @ w_p) * gate).astype(h.dtype)


def setup_inputs(seed: int = 0) -> dict:
    key = jax.random.key(seed)
    ks = iter(jax.random.split(key, 40))
    D = D_MODEL
    na1 = N_A - 1

    def nrm(shape, scale):
        return scale * jax.random.normal(next(ks), shape, jnp.float32)

    def gain(shape):
        return 1.0 + nrm(shape, 0.02)

    return {
        "x": nrm((BATCH, SEQ, D), 1.0),
        "p": nrm((DEPTH, BATCH, SEQ, PLE_DIM), 1.0),
        "a_ln_g": gain((N_A, D)),
        "a_mu": jax.random.uniform(next(ks), (N_A, 6, D), jnp.float32),
        "a_w_rkvg": nrm((N_A, 4, D, RW_WIDTH), D ** -0.5),
        "a_w0": jax.random.uniform(next(ks), (N_A, RW_WIDTH), jnp.float32, -6.0, 1.0),
        "a_w1": nrm((N_A, D, DECAY_LORA), D ** -0.5),
        "a_w2": nrm((N_A, DECAY_LORA, RW_WIDTH), 0.3 * DECAY_LORA ** -0.5),
        "a_a0": nrm((N_A, RW_WIDTH), 0.5),
        "a_a1": nrm((N_A, D, AAA_LORA), D ** -0.5),
        "a_a2": nrm((N_A, AAA_LORA, RW_WIDTH), 0.5 * AAA_LORA ** -0.5),
        "a_v0": 1.0 + nrm((na1, RW_WIDTH), 0.1),
        "a_v1": nrm((na1, D, MV_LORA), D ** -0.5),
        "a_v2": nrm((na1, MV_LORA, RW_WIDTH), 0.5 * MV_LORA ** -0.5),
        "a_k_k": 0.85 + nrm((N_A, RW_WIDTH), 0.05),
        "a_k_a": 1.0 + nrm((N_A, RW_WIDTH), 0.05),
        "a_r_k": nrm((N_A, RW_HEADS, RW_HEAD), 0.1),
        "a_gn_g": gain((N_A, RW_WIDTH)),
        "a_gn_b": nrm((N_A, RW_WIDTH), 0.02),
        "a_w_o": nrm((N_A, RW_WIDTH, D), 0.5 * RW_WIDTH ** -0.5),
        "kv_ln_g": gain((D,)),
        "w_kv": nrm((D, 2 * ATT_WIDTH), D ** -0.5),
        "b_ln_g": gain((N_B, D)),
        "b_w_in": nrm((N_B, D, Q_WIDTH + ATT_WIDTH), D ** -0.5),
        "b_w_o": nrm((N_B, ATT_WIDTH, D), 0.5 * ATT_WIDTH ** -0.5),
        "ple_w": nrm((DEPTH, PLE_DIM, D), 0.5 * PLE_DIM ** -0.5),
        "ple_gate_ln_g": gain((DEPTH, D)),
        "ple_w_gate": nrm((DEPTH, D, D), D ** -0.5),
        "final_ln_g": gain((D,)),
    }


def reference(x, p, a_ln_g, a_mu, a_w_rkvg, a_w0, a_w1, a_w2, a_a0, a_a1, a_a2,
              a_v0, a_v1, a_v2, a_k_k, a_k_a, a_r_k, a_gn_g, a_gn_b, a_w_o,
              kv_ln_g, w_kv, b_ln_g, b_w_in, b_w_o,
              ple_w, ple_gate_ln_g, ple_w_gate, final_ln_g):
    h = x
    v_first = None
    kv_windows = None
    for i in range(DEPTH):
        if i < N_A:
            vres = None if i == 0 else (a_v0[i - 1], a_v1[i - 1], a_v2[i - 1])
            mix, v = rwkv7_time_mix(rmsnorm(h, a_ln_g[i]), v_first, a_mu[i], a_w_rkvg[i],
                                    a_w0[i], a_w1[i], a_w2[i], a_a0[i], a_a1[i], a_a2[i], vres,
                                    a_k_k[i], a_k_a[i], a_r_k[i], a_gn_g[i], a_gn_b[i], a_w_o[i])
            if i == 0:
                v_first = v
        else:
            j = i - N_A
            mix = dilated_attention_mix(rmsnorm(h, b_ln_g[j]), kv_windows, b_w_in[j], b_w_o[j])
        h = h + mix
        h = h + per_layer_embedding(h, p[i], ple_w[i], ple_gate_ln_g[i], ple_w_gate[i])
        if i == N_A - 1:
            kv_windows = shared_kv_windows(h, kv_ln_g, w_kv)
    return rmsnorm(h, final_ln_g)
```

```python
import functools

import jax
import jax.numpy as jnp
from jax import lax
from jax.experimental import pallas as pl
from jax.experimental.pallas import tpu as pltpu

D_MODEL = 1024
PLE_DIM = 256
NORM_EPS = 1e-6
GN_EPS = 64e-5
RW_HEAD = 64
ATT_HEAD = 128
KV_HEADS = 8
ATT_WIDTH = KV_HEADS * ATT_HEAD
DILATIONS = (1, 4, 16)
N_GROUPS = len(DILATIONS)
Q_WIDTH = N_GROUPS * ATT_WIDTH
BLOCK = 128
LANES = 128
CHUNK = 64
NEG_BIG = -1e30

PROJ_ROWS = 256
OUT_ROWS = 512
WKV_ROWS = 256
ATT_ROWS = 2048

_NN = (((1,), (0,)), ((), ()))
_NT = (((1,), (1,)), ((), ()))
_TN = (((0,), (0,)), ((), ()))


def _bf(x):
    return x.astype(jnp.bfloat16)


def _dot(a, b, dims=_NN):
    return lax.dot_general(_bf(a), _bf(b), dims, preferred_element_type=jnp.float32)


def _dot_hilo(a, b_bf16):
    hi = _bf(a)
    lo = _bf(a - hi.astype(jnp.float32))
    return (lax.dot_general(hi, b_bf16, _NN, preferred_element_type=jnp.float32)
            + lax.dot_general(lo, b_bf16, _NN, preferred_element_type=jnp.float32))


def _rms(x, g):
    return x * lax.rsqrt(jnp.mean(x * x, axis=-1, keepdims=True) + NORM_EPS) * g


def _sigmoid(x):
    return 1.0 / (1.0 + jnp.exp(-x))


def _params(vmem_mb, *sem):
    return pltpu.CompilerParams(dimension_semantics=sem, vmem_limit_bytes=vmem_mb * 1024 * 1024)


def _const_spec(shape):
    zeros = (0,) * len(shape)
    return pl.BlockSpec(shape, lambda *_: zeros)


def _rwkv_proj_body(has_vres, *refs):
    if has_vres:
        (h_ref, hprev_ref, vfirst_ref, g_ref, mu_ref, w_ref, w0_ref, w1_ref, w2_ref, a0_ref, a1_ref, a2_ref,
         v0_ref, v1_ref, v2_ref, r_out, wl_out, k_out, v_out, a_out, gate_out) = refs
    else:
        (h_ref, hprev_ref, g_ref, mu_ref, w_ref, w0_ref, w1_ref, w2_ref, a0_ref, a1_ref, a2_ref,
         r_out, wl_out, k_out, v_out, a_out, gate_out) = refs
    g = g_ref[...]
    x = _rms(h_ref[0], g)
    rows = x.shape[0]
    xlast = _rms(hprev_ref[0][7:8, :], g)
    xlast = jnp.where(pl.program_id(1) > 0, xlast, 0.0)
    row = lax.broadcasted_iota(jnp.int32, (rows, 1), 0)
    xprev = jnp.where(row == 0, xlast, pltpu.roll(x, 1, axis=0))
    xx = xprev - x

    def mix(n):
        return _bf(x + xx * mu_ref[n:n + 1, :])

    r_out[0] = jnp.dot(mix(0), w_ref[0], preferred_element_type=jnp.float32)
    k_out[0] = jnp.dot(mix(1), w_ref[1], preferred_element_type=jnp.float32)
    xv = mix(2)
    v = jnp.dot(xv, w_ref[2], preferred_element_type=jnp.float32)
    if has_vres:
        vmix = _sigmoid(v0_ref[...] + _dot(jnp.dot(xv, v1_ref[...], preferred_element_type=jnp.float32),
                                           v2_ref[...]))
        v = v + (vfirst_ref[0] - v) * vmix
    v_out[0] = v
    gate_out[0] = jnp.dot(mix(3), w_ref[3], preferred_element_type=jnp.float32)
    lw = jnp.tanh(jnp.dot(mix(4), w1_ref[...], preferred_element_type=jnp.float32))
    z = -(w0_ref[...] + _dot(lw, w2_ref[...]))
    softplus = jnp.maximum(z, 0.0) + jnp.log(1.0 + jnp.exp(-jnp.abs(z)))
    wl_out[0] = -jnp.exp(-softplus - 0.5)
    la = jnp.dot(mix(5), a1_ref[...], preferred_element_type=jnp.float32)
    a_out[0] = _sigmoid(a0_ref[...] + _dot(la, a2_ref[...]))


def _rwkv_proj(h, v_first, ln_g, mu, w_rkvg, w0, w1, w2, a0, a1, a2, vres):
    B, S, D = h.shape
    rows = PROJ_ROWS
    has_vres = vres is not None
    tile = pl.BlockSpec((1, rows, D), lambda b, s: (b, s, 0))
    prev = pl.BlockSpec((1, 8, D), lambda b, s: (b, jnp.maximum(s * (rows // 8) - 1, 0), 0))
    row = lambda t: t.reshape(1, -1)
    args = [h, h]
    specs = [tile, prev]
    if has_vres:
        args.append(v_first)
        specs.append(tile)
    weights = [row(ln_g), mu, _bf(w_rkvg), row(w0), _bf(w1), _bf(w2), row(a0), _bf(a1), _bf(a2)]
    if has_vres:
        weights += [row(vres[0]), _bf(vres[1]), _bf(vres[2])]
    args += weights
    specs += [_const_spec(w.shape) for w in weights]
    out = jax.ShapeDtypeStruct((B, S, D), jnp.float32)
    return pl.pallas_call(
        functools.partial(_rwkv_proj_body, has_vres),
        grid=(B, S // rows),
        in_specs=specs,
        out_specs=[tile] * 6,
        out_shape=[out] * 6,
        compiler_params=_params(56, "parallel", "arbitrary"),
        name="rwkv_proj_vres" if has_vres else "rwkv_proj",
    )(*args)


def _wkv_body(r_ref, wl_ref, k_ref, v_ref, a_ref, kk_ref, ka_ref, rk_ref, gng_ref, gnb_ref, out_ref, state_ref):
    C = CHUNK
    n_pairs = r_ref.shape[2] // LANES
    n_chunks = r_ref.shape[1] // C

    @pl.when(pl.program_id(1) == 0)
    def _():
        state_ref[...] = jnp.zeros_like(state_ref)

    lane = lax.broadcasted_iota(jnp.int32, (1, LANES), 1)
    head0 = lane < RW_HEAD
    rowc = lax.broadcasted_iota(jnp.int32, (C, LANES), 0)
    colc = lax.broadcasted_iota(jnp.int32, (C, LANES), 1) % RW_HEAD
    strict = rowc > colc
    incl = rowc >= colc
    eye_pair = (rowc == colc).astype(jnp.float32)
    rr = lax.broadcasted_iota(jnp.int32, (LANES, LANES), 0)
    cc = lax.broadcasted_iota(jnp.int32, (LANES, LANES), 1)
    same_head = (rr // RW_HEAD) == (cc // RW_HEAD)
    ones_bd = jnp.where(same_head, 1.0, 0.0).astype(jnp.bfloat16)
    eye_full = (rr == cc).astype(jnp.float32)
    tr = lax.broadcasted_iota(jnp.int32, (C, C), 0)
    tc = lax.broadcasted_iota(jnp.int32, (C, C), 1)
    tri = jnp.where(tr >= tc, 1.0, 0.0).astype(jnp.bfloat16)

    def bstack(x):
        xb = _bf(x)
        zero = jnp.zeros_like(xb)
        return jnp.concatenate([jnp.where(head0, xb, zero), jnp.where(head0, zero, xb)], axis=0)

    def pmm(x_pair, y):
        return lax.dot_general(_bf(x_pair), bstack(y), _NN, preferred_element_type=jnp.float32)

    def chunk(c, carry):
        t0 = pl.multiple_of(c * C, C)
        wl = wl_ref[0, pl.ds(t0, C), :]
        w1 = _bf(wl)
        rem = wl - w1.astype(jnp.float32)
        w2 = _bf(rem)
        w3 = _bf(rem - w2.astype(jnp.float32))
        cw_all = (jnp.dot(tri, w1, preferred_element_type=jnp.float32)
                  + jnp.dot(tri, w2, preferred_element_type=jnp.float32)
                  + jnp.dot(tri, w3, preferred_element_type=jnp.float32))
        for p in range(n_pairs):
            ls = slice(p * LANES, (p + 1) * LANES)
            r = r_ref[0, pl.ds(t0, C), ls]
            k = k_ref[0, pl.ds(t0, C), ls]
            v = v_ref[0, pl.ds(t0, C), ls]
            a = a_ref[0, pl.ds(t0, C), ls]
            w = wl[:, ls]
            cw = cw_all[:, ls]
            kk = k * kk_ref[:, ls]
            ss = _dot_hilo(kk * kk, ones_bd)
            kk = kk * lax.rsqrt(jnp.maximum(ss, 1e-24))
            k = k * (1.0 + (a - 1.0) * ka_ref[:, ls])
            wc = cw[C - 1:C, :]
            r_t = r * jnp.exp(cw)
            a_t = -kk * jnp.exp(cw - w)
            e_end = jnp.exp(wc - cw)
            b_e = kk * a * e_end
            k_e = k * e_end
            e_up = jnp.exp(-wc)
            zero = jnp.zeros_like(r_t)
            a_u = a_t * e_up
            r_u = r_t * e_up
            lhs = jnp.concatenate([jnp.where(head0, a_u, zero), jnp.where(head0, r_u, zero),
                                   jnp.where(head0, zero, a_u), jnp.where(head0, zero, r_u)], axis=0)
            rhs = jnp.concatenate([b_e, k_e], axis=0)
            am = _dot(lhs, rhs, _NT)
            a_ab = jnp.concatenate([am[0:C, 0:C], am[2 * C:3 * C, 0:C]], axis=1)
            a_ak = jnp.concatenate([am[0:C, C:2 * C], am[2 * C:3 * C, C:2 * C]], axis=1)
            a_rb = jnp.concatenate([am[C:2 * C, 0:C], am[3 * C:4 * C, 0:C]], axis=1)
            a_rk = jnp.concatenate([am[C:2 * C, C:2 * C], am[3 * C:4 * C, C:2 * C]], axis=1)
            lmat = jnp.where(strict, a_ab, 0.0)
            a_ak = jnp.where(strict, a_ak, 0.0)
            a_rb = jnp.where(incl, a_rb, 0.0)
            a_rk = jnp.where(incl, a_rk, 0.0)
            tmat = eye_pair + lmat
            pw = lmat
            for _ in range(5):
                pw = pmm(pw, pw)
                tmat = tmat + pmm(tmat, pw)
            x = pmm(a_ak, v)
            a_p = pmm(tmat, a_t)
            u_l = pmm(tmat, x)
            r_p = r_t + pmm(a_rb, a_p)
            y_l = pmm(a_rb, u_l) + pmm(a_rk, v)
            m_d = eye_full * jnp.exp(wc) + jnp.where(same_head, _dot(b_e, a_p, _TN), 0.0)
            g_d = jnp.where(same_head, _dot(b_e, u_l, _TN) + _dot(k_e, v, _TN), 0.0)
            s_d = _bf(state_ref[p])
            y = lax.dot_general(_bf(r_p), s_d, _NN, preferred_element_type=jnp.float32) + y_l
            state_ref[p] = lax.dot_general(_bf(m_d), s_d, _NN, preferred_element_type=jnp.float32) + g_d
            inv_n = 1.0 / RW_HEAD
            yc = y - _dot_hilo(y, ones_bd) * inv_n
            var = _dot_hilo(yc * yc, ones_bd) * inv_n
            yn = yc * lax.rsqrt(var + GN_EPS) * gng_ref[:, ls] + gnb_ref[:, ls]
            bonus = _dot_hilo(r * k * rk_ref[:, ls], ones_bd) * v
            out_ref[0, pl.ds(t0, C), ls] = yn + bonus
        return carry

    lax.fori_loop(0, n_chunks, chunk, 0)


def _wkv(r, wl, k, v, a, k_k, k_a, r_k, gn_g, gn_b):
    B, S, D = r.shape
    rows = WKV_ROWS
    tile = pl.BlockSpec((1, rows, D), lambda b, s: (b, s, 0))
    vec = _const_spec((1, D))
    row = lambda t: t.reshape(1, -1)
    return pl.pallas_call(
        _wkv_body,
        grid=(B, S // rows),
        in_specs=[tile] * 5 + [vec] * 5,
        out_specs=tile,
        out_shape=jax.ShapeDtypeStruct((B, S, D), jnp.float32),
        scratch_shapes=[pltpu.VMEM((D // LANES, LANES, LANES), jnp.float32)],
        compiler_params=_params(48, "parallel", "arbitrary"),
        name="wkv7_chunked",
    )(r, wl, k, v, a, row(k_k), row(k_a), row(r_k), row(gn_g), row(gn_b))


def _out_body(mode, *refs):
    if mode == "kv":
        (z_ref, gate_ref, h_ref, p_ref, wo_ref, pw_ref, pg_ref, pwg_ref, kvg_ref, wkv_ref,
         h_out, k_out, v_out) = refs
    elif mode == "final":
        (z_ref, gate_ref, h_ref, p_ref, wo_ref, pw_ref, pg_ref, pwg_ref, fg_ref, h_out) = refs
    else:
        (z_ref, gate_ref, h_ref, p_ref, wo_ref, pw_ref, pg_ref, pwg_ref, h_out) = refs
    gate = gate_ref[0]
    zg = z_ref[0] * (gate * _sigmoid(gate))
    h = h_ref[0] + _dot(zg, wo_ref[...])
    pgate = _sigmoid(_dot(_rms(h, pg_ref[...]), pwg_ref[...]))
    h = h + _dot(p_ref[0], pw_ref[...]) * pgate
    if mode == "final":
        h_out[0] = _rms(h, fg_ref[...])
    else:
        h_out[0] = h
    if mode == "kv":
        kv = _dot(_rms(h, kvg_ref[...]), wkv_ref[...])
        k_out[0] = kv[:, :ATT_WIDTH]
        v_out[0] = kv[:, ATT_WIDTH:]


def _out_layer(z, gate, h, p_i, w_o, ple_w, ple_g, ple_wg, mode="plain", extra=()):
    B, S, D = h.shape
    rows = OUT_ROWS
    tile = pl.BlockSpec((1, rows, D), lambda b, s: (b, s, 0))
    ptile = pl.BlockSpec((1, rows, PLE_DIM), lambda b, s: (b, s, 0))
    weights = [_bf(w_o), _bf(ple_w), ple_g.reshape(1, -1), _bf(ple_wg)]
    if mode == "kv":
        weights += [extra[0].reshape(1, -1), _bf(extra[1])]
    elif mode == "final":
        weights += [extra[0].reshape(1, -1)]
    out = jax.ShapeDtypeStruct((B, S, D), jnp.float32)
    n_out = 3 if mode == "kv" else 1
    res = pl.pallas_call(
        functools.partial(_out_body, mode),
        grid=(B, S // rows),
        in_specs=[tile, tile, tile, ptile] + [_const_spec(w.shape) for w in weights],
        out_specs=[tile] * n_out,
        out_shape=[out] * n_out,
        compiler_params=_params(56, "parallel", "arbitrary"),
        name="out_layer_" + mode,
    )(z, gate, h, p_i, *weights)
    return res if mode == "kv" else res[0]


def _qproj_body(h_ref, g_ref, w_ref, q_out, gate_out):
    proj = jnp.dot(_bf(_rms(h_ref[0], g_ref[...])), w_ref[...], preferred_element_type=jnp.float32)
    q_out[0] = proj[:, :Q_WIDTH] * (ATT_HEAD ** -0.5)
    gate_out[0] = proj[:, Q_WIDTH:]


def _qproj(h, ln_g, w_in):
    B, S, D = h.shape
    rows = OUT_ROWS
    tile = pl.BlockSpec((1, rows, D), lambda b, s: (b, s, 0))
    qtile = pl.BlockSpec((1, rows, Q_WIDTH), lambda b, s: (b, s, 0))
    return pl.pallas_call(
        _qproj_body,
        grid=(B, S // rows),
        in_specs=[tile, _const_spec((1, D)), _const_spec(w_in.shape)],
        out_specs=[qtile, tile],
        out_shape=[jax.ShapeDtypeStruct((B, S, Q_WIDTH), jnp.float32),
                   jax.ShapeDtypeStruct((B, S, ATT_WIDTH), jnp.float32)],
        compiler_params=_params(56, "parallel", "arbitrary"),
        name="att_qproj",
    )(h, ln_g.reshape(1, -1), _bf(w_in))


def _att_body(q0_ref, q1_ref, q2_ref, k_ref, v_ref, out_ref, acc_ref, l_ref, m_ref):
    L = BLOCK
    head = pl.program_id(1)
    tile = pl.program_id(2)
    rows = out_ref.shape[1]
    t0 = tile * rows
    slope = jnp.exp2(-(head + 1).astype(jnp.float32))
    qi = lax.broadcasted_iota(jnp.int32, (L, 2 * L), 0)
    kj = lax.broadcasted_iota(jnp.int32, (L, 2 * L), 1)
    delta = L + qi - kj
    valid = (delta >= 0) & (delta <= L)
    bias = jnp.where(valid, -slope * delta.astype(jnp.float32), NEG_BIG)
    bias_first = jnp.where(kj >= L, bias, NEG_BIG)
    ones = jnp.ones((2 * L, ATT_HEAD), jnp.bfloat16)

    for g, (q_ref, d) in enumerate(zip((q0_ref, q1_ref, q2_ref), DILATIONS)):
        n_units = rows // L

        def unit(u, carry, g=g, q_ref=q_ref, d=d):
            c = u // d
            res = u % d
            loc = c * (L * d) + res
            cur = t0 + loc
            has_prev = cur >= L * d
            prv = jnp.maximum(cur - L * d, 0)
            q = q_ref[0, pl.ds(loc, L, stride=d), :]
            kcat = jnp.concatenate([k_ref[0, pl.ds(prv, L, stride=d), :],
                                    k_ref[0, pl.ds(cur, L, stride=d), :]], axis=0)
            vcat = jnp.concatenate([v_ref[0, pl.ds(prv, L, stride=d), :],
                                    v_ref[0, pl.ds(cur, L, stride=d), :]], axis=0)
            s = _dot(q, kcat, _NT) + jnp.where(has_prev, bias, bias_first)
            m = jnp.max(s, axis=-1, keepdims=True)
            p = jnp.exp(s - m)
            v_aug = jnp.concatenate([_bf(vcat), ones], axis=1)
            pv = lax.dot_general(_bf(p), v_aug, _NN, preferred_element_type=jnp.float32)
            acc_ref[g, pl.ds(loc, L, stride=d), :] = pv[:, :ATT_HEAD]
            l_ref[g, pl.ds(loc, L, stride=d), :] = pv[:, ATT_HEAD:]
            m_ref[g, pl.ds(loc, L, stride=d), :] = jnp.broadcast_to(m, (L, ATT_HEAD))
            return carry

        lax.fori_loop(0, n_units, unit, 0)

    m_all = jnp.maximum(jnp.maximum(m_ref[0], m_ref[1]), m_ref[2])
    num = jnp.zeros((rows, ATT_HEAD), jnp.float32)
    den = jnp.zeros((rows, ATT_HEAD), jnp.float32)
    for g in range(N_GROUPS):
        e = jnp.exp(m_ref[g] - m_all)
        num = num + e * acc_ref[g]
        den = den + e * l_ref[g]
    out_ref[0] = num / den


def _attention(q, k, v):
    B, S, _ = k.shape
    rows = ATT_ROWS
    q_specs = [pl.BlockSpec((1, rows, ATT_HEAD), functools.partial(lambda g, b, h, t: (b, t, g * KV_HEADS + h), g))
               for g in range(N_GROUPS)]
    kv_spec = pl.BlockSpec((1, S, ATT_HEAD), lambda b, h, t: (b, 0, h))
    scratch = pltpu.VMEM((N_GROUPS, rows, ATT_HEAD), jnp.float32)
    return pl.pallas_call(
        _att_body,
        grid=(B, KV_HEADS, S // rows),
        in_specs=q_specs + [kv_spec, kv_spec],
        out_specs=pl.BlockSpec((1, rows, ATT_HEAD), lambda b, h, t: (b, t, h)),
        out_shape=jax.ShapeDtypeStruct((B, S, ATT_WIDTH), jnp.float32),
        scratch_shapes=[scratch, scratch, scratch],
        compiler_params=_params(48, "parallel", "parallel", "arbitrary"),
        name="dilated_attention",
    )(q, q, q, k, v)


def kernel(x, p, a_ln_g, a_mu, a_w_rkvg, a_w0, a_w1, a_w2, a_a0, a_a1, a_a2, a_v0, a_v1, a_v2, a_k_k, a_k_a, a_r_k, a_gn_g, a_gn_b, a_w_o, kv_ln_g, w_kv, b_ln_g, b_w_in, b_w_o, ple_w, ple_gate_ln_g, ple_w_gate, final_ln_g):
    n_a = a_ln_g.shape[0]
    depth = p.shape[0]
    assert x.shape[1] % ATT_ROWS == 0 and x.shape[2] == D_MODEL
    h = x
    v_first = None
    k_att = v_att = None
    for i in range(depth):
        if i < n_a:
            vres = None if i == 0 else (a_v0[i - 1], a_v1[i - 1], a_v2[i - 1])
            r, wl, k, v, a, gate = _rwkv_proj(h, v_first, a_ln_g[i], a_mu[i], a_w_rkvg[i], a_w0[i], a_w1[i], a_w2[i],
                                              a_a0[i], a_a1[i], a_a2[i], vres)
            if i == 0:
                v_first = v
            z = _wkv(r, wl, k, v, a, a_k_k[i], a_k_a[i], a_r_k[i], a_gn_g[i], a_gn_b[i])
            w_o = a_w_o[i]
        else:
            j = i - n_a
            q, gate = _qproj(h, b_ln_g[j], b_w_in[j])
            z = _attention(q, k_att, v_att)
            w_o = b_w_o[j]
        if i == n_a - 1:
            h, k_att, v_att = _out_layer(z, gate, h, p[i], w_o, ple_w[i], ple_gate_ln_g[i], ple_w_gate[i],
                                         "kv", (kv_ln_g, w_kv))
        elif i == depth - 1:
            h = _out_layer(z, gate, h, p[i], w_o, ple_w[i], ple_gate_ln_g[i], ple_w_gate[i], "final", (final_ln_g,))
        else:
            h = _out_layer(z, gate, h, p[i], w_o, ple_w[i], ple_gate_ln_g[i], ple_w_gate[i])
    return h
```

```python
import functools

import jax
import jax.numpy as jnp
from jax import lax
from jax.experimental import pallas as pl
from jax.experimental.pallas import tpu as pltpu

D_MODEL = 1024
PLE_DIM = 256
NORM_EPS = 1e-6
GN_EPS = 64e-5
RW_HEAD = 64
ATT_HEAD = 128
KV_HEADS = 8
ATT_WIDTH = KV_HEADS * ATT_HEAD
DILATIONS = (1, 4, 16)
N_GROUPS = len(DILATIONS)
Q_WIDTH = N_GROUPS * ATT_WIDTH
BLOCK = 128
LANES = 128
CHUNK = 64
NEG_BIG = -1e30

PROJ_ROWS = 256
OUT_ROWS = 512
WKV_ROWS = 256
ATT_ROWS = 2048
ATT_UNROLL = 8

_NN = (((1,), (0,)), ((), ()))
_NT = (((1,), (1,)), ((), ()))
_TN = (((0,), (0,)), ((), ()))


def _bf(x):
    return x.astype(jnp.bfloat16)


def _dot(a, b, dims=_NN):
    return lax.dot_general(_bf(a), _bf(b), dims, preferred_element_type=jnp.float32)


def _dot_hilo(a, b_bf16):
    hi = _bf(a)
    lo = _bf(a - hi.astype(jnp.float32))
    return (lax.dot_general(hi, b_bf16, _NN, preferred_element_type=jnp.float32)
            + lax.dot_general(lo, b_bf16, _NN, preferred_element_type=jnp.float32))


def _rms(x, g):
    return x * lax.rsqrt(jnp.mean(x * x, axis=-1, keepdims=True) + NORM_EPS) * g


def _sigmoid(x):
    return 1.0 / (1.0 + jnp.exp(-x))


def _params(vmem_mb, *sem):
    return pltpu.CompilerParams(dimension_semantics=sem, vmem_limit_bytes=vmem_mb * 1024 * 1024)


def _const_spec(shape):
    zeros = (0,) * len(shape)
    return pl.BlockSpec(shape, lambda *_: zeros)


def _rwkv_proj_body(has_vres, *refs):
    if has_vres:
        (h_ref, hprev_ref, vfirst_ref, g_ref, mu_ref, w_ref, w0_ref, w1_ref, w2_ref, a0_ref, a1_ref, a2_ref,
         v0_ref, v1_ref, v2_ref, r_out, wl_out, k_out, v_out, a_out, gate_out) = refs
    else:
        (h_ref, hprev_ref, g_ref, mu_ref, w_ref, w0_ref, w1_ref, w2_ref, a0_ref, a1_ref, a2_ref,
         r_out, wl_out, k_out, v_out, a_out, gate_out) = refs
    g = g_ref[...]
    x = _rms(h_ref[0], g)
    rows = x.shape[0]
    xlast = _rms(hprev_ref[0][7:8, :], g)
    xlast = jnp.where(pl.program_id(1) > 0, xlast, 0.0)
    row = lax.broadcasted_iota(jnp.int32, (rows, 1), 0)
    xprev = jnp.where(row == 0, xlast, pltpu.roll(x, 1, axis=0))
    xx = xprev - x

    def mix(n):
        return _bf(x + xx * mu_ref[n:n + 1, :])

    r_out[0] = jnp.dot(mix(0), w_ref[0], preferred_element_type=jnp.float32)
    k_out[0] = jnp.dot(mix(1), w_ref[1], preferred_element_type=jnp.float32)
    xv = mix(2)
    v = jnp.dot(xv, w_ref[2], preferred_element_type=jnp.float32)
    if has_vres:
        vmix = _sigmoid(v0_ref[...] + _dot(jnp.dot(xv, v1_ref[...], preferred_element_type=jnp.float32),
                                           v2_ref[...]))
        v = v + (vfirst_ref[0] - v) * vmix
    v_out[0] = v
    gate_out[0] = jnp.dot(mix(3), w_ref[3], preferred_element_type=jnp.float32)
    lw = jnp.tanh(jnp.dot(mix(4), w1_ref[...], preferred_element_type=jnp.float32))
    z = -(w0_ref[...] + _dot(lw, w2_ref[...]))
    softplus = jnp.maximum(z, 0.0) + jnp.log(1.0 + jnp.exp(-jnp.abs(z)))
    wl_out[0] = -jnp.exp(-softplus - 0.5)
    la = jnp.dot(mix(5), a1_ref[...], preferred_element_type=jnp.float32)
    a_out[0] = _sigmoid(a0_ref[...] + _dot(la, a2_ref[...]))


def _rwkv_proj(h, v_first, ln_g, mu, w_rkvg, w0, w1, w2, a0, a1, a2, vres):
    B, S, D = h.shape
    rows = PROJ_ROWS
    has_vres = vres is not None
    tile = pl.BlockSpec((1, rows, D), lambda b, s: (b, s, 0))
    prev = pl.BlockSpec((1, 8, D), lambda b, s: (b, jnp.maximum(s * (rows // 8) - 1, 0), 0))
    row = lambda t: t.reshape(1, -1)
    args = [h, h]
    specs = [tile, prev]
    if has_vres:
        args.append(v_first)
        specs.append(tile)
    weights = [row(ln_g), mu, _bf(w_rkvg), row(w0), _bf(w1), _bf(w2), row(a0), _bf(a1), _bf(a2)]
    if has_vres:
        weights += [row(vres[0]), _bf(vres[1]), _bf(vres[2])]
    args += weights
    specs += [_const_spec(w.shape) for w in weights]
    out = jax.ShapeDtypeStruct((B, S, D), jnp.float32)
    return pl.pallas_call(
        functools.partial(_rwkv_proj_body, has_vres),
        grid=(B, S // rows),
        in_specs=specs,
        out_specs=[tile] * 6,
        out_shape=[out] * 6,
        compiler_params=_params(56, "parallel", "arbitrary"),
        name="rwkv_proj_vres" if has_vres else "rwkv_proj",
    )(*args)


def _wkv_body(r_ref, wl_ref, k_ref, v_ref, a_ref, kk_ref, ka_ref, rk_ref, gng_ref, gnb_ref, out_ref, state_ref):
    C = CHUNK
    n_pairs = r_ref.shape[2] // LANES
    n_chunks = r_ref.shape[1] // C

    @pl.when(pl.program_id(1) == 0)
    def _():
        state_ref[...] = jnp.zeros_like(state_ref)

    lane = lax.broadcasted_iota(jnp.int32, (1, LANES), 1)
    head0 = lane < RW_HEAD
    rowc = lax.broadcasted_iota(jnp.int32, (C, LANES), 0)
    colc = lax.broadcasted_iota(jnp.int32, (C, LANES), 1) % RW_HEAD
    strict = rowc > colc
    incl = rowc >= colc
    eye_pair = (rowc == colc).astype(jnp.float32)
    rr = lax.broadcasted_iota(jnp.int32, (LANES, LANES), 0)
    cc = lax.broadcasted_iota(jnp.int32, (LANES, LANES), 1)
    same_head = (rr // RW_HEAD) == (cc // RW_HEAD)
    ones_bd = jnp.where(same_head, 1.0, 0.0).astype(jnp.bfloat16)
    eye_full = (rr == cc).astype(jnp.float32)
    tr = lax.broadcasted_iota(jnp.int32, (C, C), 0)
    tc = lax.broadcasted_iota(jnp.int32, (C, C), 1)
    tri = jnp.where(tr >= tc, 1.0, 0.0).astype(jnp.bfloat16)

    def bstack(x):
        xb = _bf(x)
        zero = jnp.zeros_like(xb)
        return jnp.concatenate([jnp.where(head0, xb, zero), jnp.where(head0, zero, xb)], axis=0)

    def pmm(x_pair, y):
        return lax.dot_general(_bf(x_pair), bstack(y), _NN, preferred_element_type=jnp.float32)

    def chunk(c, carry):
        t0 = pl.multiple_of(c * C, C)
        wl = wl_ref[0, pl.ds(t0, C), :]
        w1 = _bf(wl)
        rem = wl - w1.astype(jnp.float32)
        w2 = _bf(rem)
        w3 = _bf(rem - w2.astype(jnp.float32))
        cw_all = (jnp.dot(tri, w1, preferred_element_type=jnp.float32)
                  + jnp.dot(tri, w2, preferred_element_type=jnp.float32)
                  + jnp.dot(tri, w3, preferred_element_type=jnp.float32))
        pairs = range(n_pairs)
        lanes = [slice(p * LANES, (p + 1) * LANES) for p in pairs]
        r = [r_ref[0, pl.ds(t0, C), ls] for ls in lanes]
        k = [k_ref[0, pl.ds(t0, C), ls] for ls in lanes]
        v = [v_ref[0, pl.ds(t0, C), ls] for ls in lanes]
        a = [a_ref[0, pl.ds(t0, C), ls] for ls in lanes]
        kk = [k[p] * kk_ref[:, lanes[p]] for p in pairs]
        ss = [_dot_hilo(kk[p] * kk[p], ones_bd) for p in pairs]
        kk = [kk[p] * lax.rsqrt(jnp.maximum(ss[p], 1e-24)) for p in pairs]
        k = [k[p] * (1.0 + (a[p] - 1.0) * ka_ref[:, lanes[p]]) for p in pairs]
        cw = [cw_all[:, ls] for ls in lanes]
        wc = [cw[p][C - 1:C, :] for p in pairs]
        r_t = [r[p] * jnp.exp(cw[p]) for p in pairs]
        a_t = [-kk[p] * jnp.exp(cw[p] - wl[:, lanes[p]]) for p in pairs]
        e_end = [jnp.exp(wc[p] - cw[p]) for p in pairs]
        b_e = [kk[p] * a[p] * e_end[p] for p in pairs]
        k_e = [k[p] * e_end[p] for p in pairs]
        e_up = [jnp.exp(-wc[p]) for p in pairs]
        am = [lax.dot_general(_bf(jnp.concatenate([a_t[p] * e_up[p], r_t[p] * e_up[p]], axis=0)),
                              jnp.concatenate([bstack(b_e[p]), bstack(k_e[p])], axis=0),
                              _NT, preferred_element_type=jnp.float32) for p in pairs]
        lmat = [jnp.where(strict, am[p][:C, :LANES], 0.0) for p in pairs]
        a_ak = [jnp.where(strict, am[p][:C, LANES:], 0.0) for p in pairs]
        a_rb = [jnp.where(incl, am[p][C:, :LANES], 0.0) for p in pairs]
        a_rk = [jnp.where(incl, am[p][C:, LANES:], 0.0) for p in pairs]
        x = [pmm(a_ak[p], v[p]) for p in pairs]
        tmat = [eye_pair + lmat[p] for p in pairs]
        pw = lmat
        for _ in range(5):
            pw = [pmm(pw[p], pw[p]) for p in pairs]
            tmat = [tmat[p] + pmm(tmat[p], pw[p]) for p in pairs]
        a_p = [pmm(tmat[p], a_t[p]) for p in pairs]
        u_l = [pmm(tmat[p], x[p]) for p in pairs]
        r_p = [r_t[p] + pmm(a_rb[p], a_p[p]) for p in pairs]
        y_l = [pmm(a_rb[p], u_l[p]) + pmm(a_rk[p], v[p]) for p in pairs]
        m_d = [eye_full * jnp.exp(wc[p]) + jnp.where(same_head, _dot(b_e[p], a_p[p], _TN), 0.0) for p in pairs]
        g_d = [jnp.where(same_head, _dot(b_e[p], u_l[p], _TN) + _dot(k_e[p], v[p], _TN), 0.0) for p in pairs]
        s_d = [_bf(state_ref[p]) for p in pairs]
        y = [lax.dot_general(_bf(r_p[p]), s_d[p], _NN, preferred_element_type=jnp.float32) + y_l[p] for p in pairs]
        for p in pairs:
            state_ref[p] = lax.dot_general(_bf(m_d[p]), s_d[p], _NN, preferred_element_type=jnp.float32) + g_d[p]
        inv_n = 1.0 / RW_HEAD
        yc = [y[p] - _dot_hilo(y[p], ones_bd) * inv_n for p in pairs]
        var = [_dot_hilo(yc[p] * yc[p], ones_bd) * inv_n for p in pairs]
        bonus = [_dot_hilo(r[p] * k[p] * rk_ref[:, lanes[p]], ones_bd) * v[p] for p in pairs]
        for p in pairs:
            yn = yc[p] * lax.rsqrt(var[p] + GN_EPS) * gng_ref[:, lanes[p]] + gnb_ref[:, lanes[p]]
            out_ref[0, pl.ds(t0, C), lanes[p]] = yn + bonus[p]
        return carry

    lax.fori_loop(0, n_chunks, chunk, 0)


def _wkv(r, wl, k, v, a, k_k, k_a, r_k, gn_g, gn_b):
    B, S, D = r.shape
    rows = WKV_ROWS
    tile = pl.BlockSpec((1, rows, D), lambda b, s: (b, s, 0))
    vec = _const_spec((1, D))
    row = lambda t: t.reshape(1, -1)
    return pl.pallas_call(
        _wkv_body,
        grid=(B, S // rows),
        in_specs=[tile] * 5 + [vec] * 5,
        out_specs=tile,
        out_shape=jax.ShapeDtypeStruct((B, S, D), jnp.float32),
        scratch_shapes=[pltpu.VMEM((D // LANES, LANES, LANES), jnp.float32)],
        compiler_params=_params(48, "parallel", "arbitrary"),
        name="wkv7_chunked",
    )(r, wl, k, v, a, row(k_k), row(k_a), row(r_k), row(gn_g), row(gn_b))


def _out_body(mode, *refs):
    if mode == "kv":
        (z_ref, gate_ref, h_ref, p_ref, wo_ref, pw_ref, pg_ref, pwg_ref, kvg_ref, wkv_ref,
         h_out, k_out, v_out) = refs
    elif mode == "final":
        (z_ref, gate_ref, h_ref, p_ref, wo_ref, pw_ref, pg_ref, pwg_ref, fg_ref, h_out) = refs
    else:
        (z_ref, gate_ref, h_ref, p_ref, wo_ref, pw_ref, pg_ref, pwg_ref, h_out) = refs
    gate = gate_ref[0]
    zg = z_ref[0] * (gate * _sigmoid(gate))
    h = h_ref[0] + _dot(zg, wo_ref[...])
    pgate = _sigmoid(_dot(_rms(h, pg_ref[...]), pwg_ref[...]))
    h = h + _dot(p_ref[0], pw_ref[...]) * pgate
    if mode == "final":
        h_out[0] = _rms(h, fg_ref[...])
    else:
        h_out[0] = h
    if mode == "kv":
        kv = _dot(_rms(h, kvg_ref[...]), wkv_ref[...])
        k_out[0] = kv[:, :ATT_WIDTH]
        v_out[0] = kv[:, ATT_WIDTH:]


def _out_layer(z, gate, h, p_i, w_o, ple_w, ple_g, ple_wg, mode="plain", extra=()):
    B, S, D = h.shape
    rows = OUT_ROWS
    tile = pl.BlockSpec((1, rows, D), lambda b, s: (b, s, 0))
    ptile = pl.BlockSpec((1, rows, PLE_DIM), lambda b, s: (b, s, 0))
    weights = [_bf(w_o), _bf(ple_w), ple_g.reshape(1, -1), _bf(ple_wg)]
    if mode == "kv":
        weights += [extra[0].reshape(1, -1), _bf(extra[1])]
    elif mode == "final":
        weights += [extra[0].reshape(1, -1)]
    out = jax.ShapeDtypeStruct((B, S, D), jnp.float32)
    n_out = 3 if mode == "kv" else 1
    res = pl.pallas_call(
        functools.partial(_out_body, mode),
        grid=(B, S // rows),
        in_specs=[tile, tile, tile, ptile] + [_const_spec(w.shape) for w in weights],
        out_specs=[tile] * n_out,
        out_shape=[out] * n_out,
        compiler_params=_params(56, "parallel", "arbitrary"),
        name="out_layer_" + mode,
    )(z, gate, h, p_i, *weights)
    return res if mode == "kv" else res[0]


def _qproj_body(h_ref, g_ref, w_ref, q_out, gate_out):
    proj = jnp.dot(_bf(_rms(h_ref[0], g_ref[...])), w_ref[...], preferred_element_type=jnp.float32)
    q_out[0] = proj[:, :Q_WIDTH] * (ATT_HEAD ** -0.5)
    gate_out[0] = proj[:, Q_WIDTH:]


def _qproj(h, ln_g, w_in):
    B, S, D = h.shape
    rows = OUT_ROWS
    tile = pl.BlockSpec((1, rows, D), lambda b, s: (b, s, 0))
    qtile = pl.BlockSpec((1, rows, Q_WIDTH), lambda b, s: (b, s, 0))
    return pl.pallas_call(
        _qproj_body,
        grid=(B, S // rows),
        in_specs=[tile, _const_spec((1, D)), _const_spec(w_in.shape)],
        out_specs=[qtile, tile],
        out_shape=[jax.ShapeDtypeStruct((B, S, Q_WIDTH), jnp.float32),
                   jax.ShapeDtypeStruct((B, S, ATT_WIDTH), jnp.float32)],
        compiler_params=_params(56, "parallel", "arbitrary"),
        name="att_qproj",
    )(h, ln_g.reshape(1, -1), _bf(w_in))


def _att_body(q0_ref, q1_ref, q2_ref, k_ref, v_ref, out_ref, acc_ref, l_ref, m_ref):
    L = BLOCK
    head = pl.program_id(1)
    tile = pl.program_id(2)
    rows = out_ref.shape[1]
    t0 = tile * rows
    slope = jnp.exp2(-(head + 1).astype(jnp.float32))
    qi = lax.broadcasted_iota(jnp.int32, (L, 2 * L), 0)
    kj = lax.broadcasted_iota(jnp.int32, (L, 2 * L), 1)
    delta = L + qi - kj
    valid = (delta >= 0) & (delta <= L)
    bias = jnp.where(valid, -slope * delta.astype(jnp.float32), NEG_BIG)
    bias_first = jnp.where(kj >= L, bias, NEG_BIG)
    ones = jnp.ones((2 * L, ATT_HEAD), jnp.bfloat16)

    for g, (q_ref, d) in enumerate(zip((q0_ref, q1_ref, q2_ref), DILATIONS)):
        n_units = rows // L

        def units(i, carry, g=g, q_ref=q_ref, d=d):
            us = [i * ATT_UNROLL + j for j in range(ATT_UNROLL)]
            loc = [(u // d) * (L * d) + u % d for u in us]
            cur = [t0 + x for x in loc]
            prv = [jnp.maximum(x - L * d, 0) for x in cur]
            q = [_bf(q_ref[0, pl.ds(x, L, stride=d), :]) for x in loc]
            kcat = [_bf(jnp.concatenate([k_ref[0, pl.ds(pv_, L, stride=d), :],
                                         k_ref[0, pl.ds(cu, L, stride=d), :]], axis=0))
                    for pv_, cu in zip(prv, cur)]
            v_aug = [jnp.concatenate([_bf(jnp.concatenate([v_ref[0, pl.ds(pv_, L, stride=d), :],
                                                           v_ref[0, pl.ds(cu, L, stride=d), :]], axis=0)),
                                      ones], axis=1)
                     for pv_, cu in zip(prv, cur)]
            s = [lax.dot_general(q_, k_, _NT, preferred_element_type=jnp.float32)
                 + jnp.where(cu >= L * d, bias, bias_first) for q_, k_, cu in zip(q, kcat, cur)]
            m = [jnp.max(s_, axis=-1, keepdims=True) for s_ in s]
            p = [_bf(jnp.exp(s_ - m_)) for s_, m_ in zip(s, m)]
            pv = [lax.dot_general(p_, v_, _NN, preferred_element_type=jnp.float32) for p_, v_ in zip(p, v_aug)]
            for x, pv_, m_ in zip(loc, pv, m):
                acc_ref[g, pl.ds(x, L, stride=d), :] = pv_[:, :ATT_HEAD]
                l_ref[g, pl.ds(x, L, stride=d), :] = pv_[:, ATT_HEAD:]
                m_ref[g, pl.ds(x, L, stride=d), :] = jnp.broadcast_to(m_, (L, ATT_HEAD))
            return carry

        lax.fori_loop(0, n_units // ATT_UNROLL, units, 0)

    m_all = jnp.maximum(jnp.maximum(m_ref[0], m_ref[1]), m_ref[2])
    num = jnp.zeros((rows, ATT_HEAD), jnp.float32)
    den = jnp.zeros((rows, ATT_HEAD), jnp.float32)
    for g in range(N_GROUPS):
        e = jnp.exp(m_ref[g] - m_all)
        num = num + e * acc_ref[g]
        den = den + e * l_ref[g]
    out_ref[0] = num / den


def _attention(q, k, v):
    B, S, _ = k.shape
    rows = ATT_ROWS
    q_specs = [pl.BlockSpec((1, rows, ATT_HEAD), functools.partial(lambda g, b, h, t: (b, t, g * KV_HEADS + h), g))
               for g in range(N_GROUPS)]
    kv_spec = pl.BlockSpec((1, S, ATT_HEAD), lambda b, h, t: (b, 0, h))
    scratch = pltpu.VMEM((N_GROUPS, rows, ATT_HEAD), jnp.float32)
    return pl.pallas_call(
        _att_body,
        grid=(B, KV_HEADS, S // rows),
        in_specs=q_specs + [kv_spec, kv_spec],
        out_specs=pl.BlockSpec((1, rows, ATT_HEAD), lambda b, h, t: (b, t, h)),
        out_shape=jax.ShapeDtypeStruct((B, S, ATT_WIDTH), jnp.float32),
        scratch_shapes=[scratch, scratch, scratch],
        compiler_params=_params(48, "parallel", "parallel", "arbitrary"),
        name="dilated_attention",
    )(q, q, q, k, v)


def kernel(x, p, a_ln_g, a_mu, a_w_rkvg, a_w0, a_w1, a_w2, a_a0, a_a1, a_a2, a_v0, a_v1, a_v2, a_k_k, a_k_a, a_r_k, a_gn_g, a_gn_b, a_w_o, kv_ln_g, w_kv, b_ln_g, b_w_in, b_w_o, ple_w, ple_gate_ln_g, ple_w_gate, final_ln_g):
    n_a = a_ln_g.shape[0]
    depth = p.shape[0]
    assert x.shape[1] % ATT_ROWS == 0 and x.shape[2] == D_MODEL
    h = x
    v_first = None
    k_att = v_att = None
    for i in range(depth):
        if i < n_a:
            vres = None if i == 0 else (a_v0[i - 1], a_v1[i - 1], a_v2[i - 1])
            r, wl, k, v, a, gate = _rwkv_proj(h, v_first, a_ln_g[i], a_mu[i], a_w_rkvg[i], a_w0[i], a_w1[i], a_w2[i],
                                              a_a0[i], a_a1[i], a_a2[i], vres)
            if i == 0:
                v_first = v
            z = _wkv(r, wl, k, v, a, a_k_k[i], a_k_a[i], a_r_k[i], a_gn_g[i], a_gn_b[i])
            w_o = a_w_o[i]
        else:
            j = i - n_a
            q, gate = _qproj(h, b_ln_g[j], b_w_in[j])
            z = _attention(q, k_att, v_att)
            w_o = b_w_o[j]
        if i == n_a - 1:
            h, k_att, v_att = _out_layer(z, gate, h, p[i], w_o, ple_w[i], ple_gate_ln_g[i], ple_w_gate[i],
                                         "kv", (kv_ln_g, w_kv))
        elif i == depth - 1:
            h = _out_layer(z, gate, h, p[i], w_o, ple_w[i], ple_gate_ln_g[i], ple_w_gate[i], "final", (final_ln_g,))
        else:
            h = _out_layer(z, gate, h, p[i], w_o, ple_w[i], ple_gate_ln_g[i], ple_w_gate[i])
    return h
```

```python
import functools

import jax
import jax.numpy as jnp
from jax import lax
from jax.experimental import pallas as pl
from jax.experimental.pallas import tpu as pltpu

D_MODEL = 1024
PLE_DIM = 256
NORM_EPS = 1e-6
GN_EPS = 64e-5
RW_HEAD = 64
ATT_HEAD = 128
KV_HEADS = 8
ATT_WIDTH = KV_HEADS * ATT_HEAD
DILATIONS = (1, 4, 16)
N_GROUPS = len(DILATIONS)
Q_WIDTH = N_GROUPS * ATT_WIDTH
BLOCK = 128
LANES = 128
CHUNK = 64
NEG_BIG = -1e30

PROJ_ROWS = 256
OUT_ROWS = 512
WKV_ROWS = 512
WKV_STAGGER = 10
ATT_ROWS = 2048
ATT_UNROLL = 8

_NN = (((1,), (0,)), ((), ()))
_NT = (((1,), (1,)), ((), ()))
_TN = (((0,), (0,)), ((), ()))


def _bf(x):
    return x.astype(jnp.bfloat16)


def _dot(a, b, dims=_NN):
    return lax.dot_general(_bf(a), _bf(b), dims, preferred_element_type=jnp.float32)


def _dot_hilo(a, b_bf16):
    hi = _bf(a)
    lo = _bf(a - hi.astype(jnp.float32))
    return (lax.dot_general(hi, b_bf16, _NN, preferred_element_type=jnp.float32)
            + lax.dot_general(lo, b_bf16, _NN, preferred_element_type=jnp.float32))


def _rms(x, g):
    return x * lax.rsqrt(jnp.mean(x * x, axis=-1, keepdims=True) + NORM_EPS) * g


def _sigmoid(x):
    return 1.0 / (1.0 + jnp.exp(-x))


def _params(vmem_mb, *sem):
    return pltpu.CompilerParams(dimension_semantics=sem, vmem_limit_bytes=vmem_mb * 1024 * 1024)


def _const_spec(shape):
    zeros = (0,) * len(shape)
    return pl.BlockSpec(shape, lambda *_: zeros)


def _rwkv_proj_body(has_vres, *refs):
    if has_vres:
        (h_ref, hprev_ref, vfirst_ref, g_ref, mu_ref, w_ref, w0_ref, w1_ref, w2_ref, a0_ref, a1_ref, a2_ref,
         v0_ref, v1_ref, v2_ref, r_out, wl_out, k_out, v_out, a_out, gate_out) = refs
    else:
        (h_ref, hprev_ref, g_ref, mu_ref, w_ref, w0_ref, w1_ref, w2_ref, a0_ref, a1_ref, a2_ref,
         r_out, wl_out, k_out, v_out, a_out, gate_out) = refs
    g = g_ref[...]
    x = _rms(h_ref[0], g)
    rows = x.shape[0]
    xlast = _rms(hprev_ref[0][7:8, :], g)
    xlast = jnp.where(pl.program_id(1) > 0, xlast, 0.0)
    row = lax.broadcasted_iota(jnp.int32, (rows, 1), 0)
    xprev = jnp.where(row == 0, xlast, pltpu.roll(x, 1, axis=0))
    xx = xprev - x

    def mix(n):
        return _bf(x + xx * mu_ref[n:n + 1, :])

    r_out[0] = jnp.dot(mix(0), w_ref[0], preferred_element_type=jnp.float32)
    k_out[0] = jnp.dot(mix(1), w_ref[1], preferred_element_type=jnp.float32)
    xv = mix(2)
    v = jnp.dot(xv, w_ref[2], preferred_element_type=jnp.float32)
    if has_vres:
        vmix = _sigmoid(v0_ref[...] + _dot(jnp.dot(xv, v1_ref[...], preferred_element_type=jnp.float32),
                                           v2_ref[...]))
        v = v + (vfirst_ref[0] - v) * vmix
    v_out[0] = v
    gate_out[0] = jnp.dot(mix(3), w_ref[3], preferred_element_type=jnp.float32)
    lw = jnp.tanh(jnp.dot(mix(4), w1_ref[...], preferred_element_type=jnp.float32))
    z = -(w0_ref[...] + _dot(lw, w2_ref[...]))
    softplus = jnp.maximum(z, 0.0) + jnp.log(1.0 + jnp.exp(-jnp.abs(z)))
    wl_out[0] = -jnp.exp(-softplus - 0.5)
    la = jnp.dot(mix(5), a1_ref[...], preferred_element_type=jnp.float32)
    a_out[0] = _sigmoid(a0_ref[...] + _dot(la, a2_ref[...]))


def _rwkv_proj(h, v_first, ln_g, mu, w_rkvg, w0, w1, w2, a0, a1, a2, vres):
    B, S, D = h.shape
    rows = PROJ_ROWS
    has_vres = vres is not None
    tile = pl.BlockSpec((1, rows, D), lambda b, s: (b, s, 0))
    prev = pl.BlockSpec((1, 8, D), lambda b, s: (b, jnp.maximum(s * (rows // 8) - 1, 0), 0))
    row = lambda t: t.reshape(1, -1)
    args = [h, h]
    specs = [tile, prev]
    if has_vres:
        args.append(v_first)
        specs.append(tile)
    weights = [row(ln_g), mu, _bf(w_rkvg), row(w0), _bf(w1), _bf(w2), row(a0), _bf(a1), _bf(a2)]
    if has_vres:
        weights += [row(vres[0]), _bf(vres[1]), _bf(vres[2])]
    args += weights
    specs += [_const_spec(w.shape) for w in weights]
    out = jax.ShapeDtypeStruct((B, S, D), jnp.float32)
    return pl.pallas_call(
        functools.partial(_rwkv_proj_body, has_vres),
        grid=(B, S // rows),
        in_specs=specs,
        out_specs=[tile] * 6,
        out_shape=[out] * 6,
        compiler_params=_params(56, "parallel", "arbitrary"),
        name="rwkv_proj_vres" if has_vres else "rwkv_proj",
    )(*args)


def _wkv_body(r_ref, wl_ref, k_ref, v_ref, a_ref, kk_ref, ka_ref, rk_ref, gng_ref, gnb_ref, out_ref, state_ref):
    C = CHUNK
    n_pairs = r_ref.shape[2] // LANES
    n_chunks = r_ref.shape[1] // C

    @pl.when(pl.program_id(1) == 0)
    def _():
        state_ref[...] = jnp.zeros_like(state_ref)

    lane = lax.broadcasted_iota(jnp.int32, (1, LANES), 1)
    head0 = lane < RW_HEAD
    rowc = lax.broadcasted_iota(jnp.int32, (C, LANES), 0)
    colc = lax.broadcasted_iota(jnp.int32, (C, LANES), 1) % RW_HEAD
    strict = rowc > colc
    incl2 = jnp.concatenate([rowc >= colc, rowc >= colc], axis=1)
    eye_pair = (rowc == colc).astype(jnp.float32)
    rr = lax.broadcasted_iota(jnp.int32, (LANES, LANES), 0)
    cc = lax.broadcasted_iota(jnp.int32, (LANES, LANES), 1)
    same_head = (rr // RW_HEAD) == (cc // RW_HEAD)
    ones_bd = jnp.where(same_head, 1.0, 0.0).astype(jnp.bfloat16)
    eye_full = (rr == cc).astype(jnp.float32)
    tr = lax.broadcasted_iota(jnp.int32, (C, C), 0)
    tc = lax.broadcasted_iota(jnp.int32, (C, C), 1)
    tri = jnp.where(tr >= tc, 1.0, 0.0).astype(jnp.bfloat16)

    def bstack(x):
        xb = _bf(x)
        zero = jnp.zeros_like(xb)
        return jnp.concatenate([jnp.where(head0, xb, zero), jnp.where(head0, zero, xb)], axis=0)

    def head_sums(xs):
        return [jnp.where(head0,
                          jnp.sum(jnp.where(head0, x, 0.0), axis=-1, keepdims=True),
                          jnp.sum(jnp.where(head0, 0.0, x), axis=-1, keepdims=True)) for x in xs]

    def pmm(x_pair, y):
        return lax.dot_general(_bf(x_pair), bstack(y), _NN, preferred_element_type=jnp.float32)

    def prefix_sum(wl):
        w1 = _bf(wl)
        rem = wl - w1.astype(jnp.float32)
        w2 = _bf(rem)
        w3 = _bf(rem - w2.astype(jnp.float32))
        return (jnp.dot(tri, w1, preferred_element_type=jnp.float32)
                + jnp.dot(tri, w2, preferred_element_type=jnp.float32)
                + jnp.dot(tri, w3, preferred_element_type=jnp.float32))

    pairs = range(n_pairs)
    lanes = [slice(p * LANES, (p + 1) * LANES) for p in pairs]
    state = [state_ref[p] for p in pairs]

    def chunk_steps(ci):
        t0 = ci * C
        wl_all = wl_ref[0, t0:t0 + C, :]
        cw_all = prefix_sum(wl_all)
        r = [r_ref[0, t0:t0 + C, ls] for ls in lanes]
        k = [k_ref[0, t0:t0 + C, ls] for ls in lanes]
        v = [v_ref[0, t0:t0 + C, ls] for ls in lanes]
        a = [a_ref[0, t0:t0 + C, ls] for ls in lanes]
        kk = [k[p] * kk_ref[:, lanes[p]] for p in pairs]
        ss = head_sums([kk[p] * kk[p] for p in pairs])
        yield
        k = [k[p] * (1.0 + (a[p] - 1.0) * ka_ref[:, lanes[p]]) for p in pairs]
        rk = head_sums([r[p] * k[p] * rk_ref[:, lanes[p]] for p in pairs])
        yield
        kk = [kk[p] * lax.rsqrt(jnp.maximum(ss[p], 1e-24)) for p in pairs]
        cw = [cw_all[:, ls] for ls in lanes]
        wc = [cw[p][C - 1:C, :] for p in pairs]
        r_t = [r[p] * jnp.exp(cw[p]) for p in pairs]
        a_t = [-kk[p] * jnp.exp(cw[p] - wl_all[:, lanes[p]]) for p in pairs]
        e_end = [jnp.exp(wc[p] - cw[p]) for p in pairs]
        b_e = [kk[p] * a[p] * e_end[p] for p in pairs]
        k_e = [k[p] * e_end[p] for p in pairs]
        e_up = [jnp.exp(-wc[p]) for p in pairs]
        am = [lax.dot_general(_bf(jnp.concatenate([a_t[p] * e_up[p], r_t[p] * e_up[p]], axis=0)),
                              jnp.concatenate([bstack(b_e[p]), bstack(k_e[p])], axis=0),
                              _NT, preferred_element_type=jnp.float32) for p in pairs]
        yield
        lmat =[jnp.where(strict, am[p][:C, :LANES], 0.0) for p in pairs]
        a_ak = [jnp.where(strict, am[p][:C, LANES:], 0.0) for p in pairs]
        a_r = [_bf(jnp.where(incl2, am[p][C:, :], 0.0)) for p in pairs]
        x = [pmm(a_ak[p], v[p]) for p in pairs]
        yield
        tmat = [eye_pair + lmat[p] for p in pairs]
        pw = lmat
        for _ in range(5):
            pw = [pmm(pw[p], pw[p]) for p in pairs]
            yield
            tmat = [tmat[p] + pmm(tmat[p], pw[p]) for p in pairs]
            yield
        apul = [lax.dot_general(_bf(tmat[p]), jnp.concatenate([bstack(a_t[p]), bstack(x[p])], axis=1),
                                _NN, preferred_element_type=jnp.float32) for p in pairs]
        yield
        vb =[bstack(v[p]) for p in pairs]
        zb = jnp.zeros((LANES, LANES), jnp.bfloat16)
        rpyl = [lax.dot_general(a_r[p],
                                jnp.concatenate([jnp.concatenate([bstack(apul[p][:, :LANES]), zb], axis=0),
                                                 jnp.concatenate([bstack(apul[p][:, LANES:]), vb[p]], axis=0)], axis=1),
                                _NN, preferred_element_type=jnp.float32) for p in pairs]
        zc = jnp.zeros((C, LANES), jnp.bfloat16)
        mg = [lax.dot_general(_bf(jnp.concatenate([b_e[p], k_e[p]], axis=0)),
                              jnp.concatenate([_bf(apul[p]), jnp.concatenate([zc, _bf(v[p])], axis=1)], axis=0),
                              _TN, preferred_element_type=jnp.float32) for p in pairs]
        yield
        r_p = [r_t[p] + rpyl[p][:, :LANES] for p in pairs]
        m_d = [eye_full * jnp.exp(wc[p]) + jnp.where(same_head, mg[p][:, :LANES], 0.0) for p in pairs]
        ys = [lax.dot_general(_bf(jnp.concatenate([r_p[p], m_d[p]], axis=0)), _bf(state[p]), _NN,
                              preferred_element_type=jnp.float32) for p in pairs]
        for p in pairs:
            state[p] = ys[p][C:, :] + jnp.where(same_head, mg[p][:, LANES:], 0.0)
        y = [ys[p][:C, :] + rpyl[p][:, LANES:] for p in pairs]
        inv_n = 1.0 / RW_HEAD
        mean = head_sums(y)
        yield
        yc = [y[p] - mean[p] * inv_n for p in pairs]
        var = head_sums([yc[p] * yc[p] for p in pairs])
        yield
        for p in pairs:
            yn = yc[p] * lax.rsqrt(var[p] * inv_n + GN_EPS) * gng_ref[:, lanes[p]] + gnb_ref[:, lanes[p]]
            out_ref[0, t0:t0 + C, lanes[p]] = yn + rk[p] * v[p]

    steps = [chunk_steps(ci) for ci in range(n_chunks)]
    live = set(range(n_chunks))
    tick = 0
    while live:
        for ci in sorted(live):
            if tick >= ci * WKV_STAGGER:
                try:
                    next(steps[ci])
                except StopIteration:
                    live.discard(ci)
        tick += 1
    for p in pairs:
        state_ref[p] = state[p]


def _wkv(r, wl, k, v, a, k_k, k_a, r_k, gn_g, gn_b):
    B, S, D = r.shape
    rows = WKV_ROWS
    tile = pl.BlockSpec((1, rows, D), lambda b, s: (b, s, 0))
    vec = _const_spec((1, D))
    row = lambda t: t.reshape(1, -1)
    return pl.pallas_call(
        _wkv_body,
        grid=(B, S // rows),
        in_specs=[tile] * 5 + [vec] * 5,
        out_specs=tile,
        out_shape=jax.ShapeDtypeStruct((B, S, D), jnp.float32),
        scratch_shapes=[pltpu.VMEM((D // LANES, LANES, LANES), jnp.float32)],
        compiler_params=_params(48, "parallel", "arbitrary"),
        name="wkv7_chunked",
    )(r, wl, k, v, a, row(k_k), row(k_a), row(r_k), row(gn_g), row(gn_b))


def _out_body(mode, *refs):
    if mode == "kv":
        (z_ref, gate_ref, h_ref, p_ref, wo_ref, pw_ref, pg_ref, pwg_ref, kvg_ref, wkv_ref,
         h_out, k_out, v_out) = refs
    elif mode == "final":
        (z_ref, gate_ref, h_ref, p_ref, wo_ref, pw_ref, pg_ref, pwg_ref, fg_ref, h_out) = refs
    else:
        (z_ref, gate_ref, h_ref, p_ref, wo_ref, pw_ref, pg_ref, pwg_ref, h_out) = refs
    gate = gate_ref[0]
    zg = z_ref[0] * (gate * _sigmoid(gate))
    h = h_ref[0] + _dot(zg, wo_ref[...])
    pgate = _sigmoid(_dot(_rms(h, pg_ref[...]), pwg_ref[...]))
    h = h + _dot(p_ref[0], pw_ref[...]) * pgate
    if mode == "final":
        h_out[0] = _rms(h, fg_ref[...])
    else:
        h_out[0] = h
    if mode == "kv":
        kv = _dot(_rms(h, kvg_ref[...]), wkv_ref[...])
        k_out[0] = kv[:, :ATT_WIDTH]
        v_out[0] = kv[:, ATT_WIDTH:]


def _out_layer(z, gate, h, p_i, w_o, ple_w, ple_g, ple_wg, mode="plain", extra=()):
    B, S, D = h.shape
    rows = OUT_ROWS
    tile = pl.BlockSpec((1, rows, D), lambda b, s: (b, s, 0))
    ptile = pl.BlockSpec((1, rows, PLE_DIM), lambda b, s: (b, s, 0))
    weights = [_bf(w_o), _bf(ple_w), ple_g.reshape(1, -1), _bf(ple_wg)]
    if mode == "kv":
        weights += [extra[0].reshape(1, -1), _bf(extra[1])]
    elif mode == "final":
        weights += [extra[0].reshape(1, -1)]
    out = jax.ShapeDtypeStruct((B, S, D), jnp.float32)
    n_out = 3 if mode == "kv" else 1
    res = pl.pallas_call(
        functools.partial(_out_body, mode),
        grid=(B, S // rows),
        in_specs=[tile, tile, tile, ptile] + [_const_spec(w.shape) for w in weights],
        out_specs=[tile] * n_out,
        out_shape=[out] * n_out,
        compiler_params=_params(56, "parallel", "arbitrary"),
        name="out_layer_" + mode,
    )(z, gate, h, p_i, *weights)
    return res if mode == "kv" else res[0]


def _qproj_body(h_ref, g_ref, w_ref, q_out, gate_out):
    proj = jnp.dot(_bf(_rms(h_ref[0], g_ref[...])), w_ref[...], preferred_element_type=jnp.float32)
    q_out[0] = proj[:, :Q_WIDTH] * (ATT_HEAD ** -0.5)
    gate_out[0] = proj[:, Q_WIDTH:]


def _qproj(h, ln_g, w_in):
    B, S, D = h.shape
    rows = OUT_ROWS
    tile = pl.BlockSpec((1, rows, D), lambda b, s: (b, s, 0))
    qtile = pl.BlockSpec((1, rows, Q_WIDTH), lambda b, s: (b, s, 0))
    return pl.pallas_call(
        _qproj_body,
        grid=(B, S // rows),
        in_specs=[tile, _const_spec((1, D)), _const_spec(w_in.shape)],
        out_specs=[qtile, tile],
        out_shape=[jax.ShapeDtypeStruct((B, S, Q_WIDTH), jnp.float32),
                   jax.ShapeDtypeStruct((B, S, ATT_WIDTH), jnp.float32)],
        compiler_params=_params(56, "parallel", "arbitrary"),
        name="att_qproj",
    )(h, ln_g.reshape(1, -1), _bf(w_in))


def _att_body(q0_ref, q1_ref, q2_ref, k_ref, v_ref, out_ref, acc_ref, l_ref, m_ref):
    L = BLOCK
    head = pl.program_id(1)
    tile = pl.program_id(2)
    rows = out_ref.shape[1]
    t0 = tile * rows
    slope = jnp.exp2(-(head + 1).astype(jnp.float32))
    qi = lax.broadcasted_iota(jnp.int32, (L, 2 * L), 0)
    kj = lax.broadcasted_iota(jnp.int32, (L, 2 * L), 1)
    delta = L + qi - kj
    valid = (delta >= 0) & (delta <= L)
    bias = jnp.where(valid, -slope * delta.astype(jnp.float32), NEG_BIG)
    bias_first = jnp.where(kj >= L, bias, NEG_BIG)
    ones = jnp.ones((2 * L, ATT_HEAD), jnp.bfloat16)

    for g, (q_ref, d) in enumerate(zip((q0_ref, q1_ref, q2_ref), DILATIONS)):
        n_units = rows // L

        def units(i, carry, g=g, q_ref=q_ref, d=d):
            us = [i * ATT_UNROLL + j for j in range(ATT_UNROLL)]
            loc = [(u // d) * (L * d) + u % d for u in us]
            cur = [t0 + x for x in loc]
            prv = [jnp.maximum(x - L * d, 0) for x in cur]
            q = [_bf(q_ref[0, pl.ds(x, L, stride=d), :]) for x in loc]
            kcat = [_bf(jnp.concatenate([k_ref[0, pl.ds(pv_, L, stride=d), :],
                                         k_ref[0, pl.ds(cu, L, stride=d), :]], axis=0))
                    for pv_, cu in zip(prv, cur)]
            v_aug = [jnp.concatenate([_bf(jnp.concatenate([v_ref[0, pl.ds(pv_, L, stride=d), :],
                                                           v_ref[0, pl.ds(cu, L, stride=d), :]], axis=0)),
                                      ones], axis=1)
                     for pv_, cu in zip(prv, cur)]
            s = [lax.dot_general(q_, k_, _NT, preferred_element_type=jnp.float32)
                 + jnp.where(cu >= L * d, bias, bias_first) for q_, k_, cu in zip(q, kcat, cur)]
            m = [jnp.max(s_, axis=-1, keepdims=True) for s_ in s]
            p = [_bf(jnp.exp(s_ - m_)) for s_, m_ in zip(s, m)]
            pv = [lax.dot_general(p_, v_, _NN, preferred_element_type=jnp.float32) for p_, v_ in zip(p, v_aug)]
            for x, pv_, m_ in zip(loc, pv, m):
                acc_ref[g, pl.ds(x, L, stride=d), :] = pv_[:, :ATT_HEAD]
                l_ref[g, pl.ds(x, L, stride=d), :] = pv_[:, ATT_HEAD:]
                m_ref[g, pl.ds(x, L, stride=d), :] = jnp.broadcast_to(m_, (L, ATT_HEAD))
            return carry

        lax.fori_loop(0, n_units // ATT_UNROLL, units, 0)

    m_all = jnp.maximum(jnp.maximum(m_ref[0], m_ref[1]), m_ref[2])
    num = jnp.zeros((rows, ATT_HEAD), jnp.float32)
    den = jnp.zeros((rows, ATT_HEAD), jnp.float32)
    for g in range(N_GROUPS):
        e = jnp.exp(m_ref[g] - m_all)
        num = num + e * acc_ref[g]
        den = den + e * l_ref[g]
    out_ref[0] = num / den


def _attention(q, k, v):
    B, S, _ = k.shape
    rows = ATT_ROWS
    q_specs = [pl.BlockSpec((1, rows, ATT_HEAD), functools.partial(lambda g, b, h, t: (b, t, g * KV_HEADS + h), g))
               for g in range(N_GROUPS)]
    kv_spec = pl.BlockSpec((1, S, ATT_HEAD), lambda b, h, t: (b, 0, h))
    scratch = pltpu.VMEM((N_GROUPS, rows, ATT_HEAD), jnp.float32)
    return pl.pallas_call(
        _att_body,
        grid=(B, KV_HEADS, S // rows),
        in_specs=q_specs + [kv_spec, kv_spec],
        out_specs=pl.BlockSpec((1, rows, ATT_HEAD), lambda b, h, t: (b, t, h)),
        out_shape=jax.ShapeDtypeStruct((B, S, ATT_WIDTH), jnp.float32),
        scratch_shapes=[scratch, scratch, scratch],
        compiler_params=_params(48, "parallel", "parallel", "arbitrary"),
        name="dilated_attention",
    )(q, q, q, k, v)


def kernel(x, p, a_ln_g, a_mu, a_w_rkvg, a_w0, a_w1, a_w2, a_a0, a_a1, a_a2, a_v0, a_v1, a_v2, a_k_k, a_k_a, a_r_k, a_gn_g, a_gn_b, a_w_o, kv_ln_g, w_kv, b_ln_g, b_w_in, b_w_o, ple_w, ple_gate_ln_g, ple_w_gate, final_ln_g):
    n_a = a_ln_g.shape[0]
    depth = p.shape[0]
    assert x.shape[1] % ATT_ROWS == 0 and x.shape[2] == D_MODEL
    h = x
    v_first = None
    k_att = v_att = None
    for i in range(depth):
        if i < n_a:
            vres = None if i == 0 else (a_v0[i - 1], a_v1[i - 1], a_v2[i - 1])
            r, wl, k, v, a, gate = _rwkv_proj(h, v_first, a_ln_g[i], a_mu[i], a_w_rkvg[i], a_w0[i], a_w1[i], a_w2[i],
                                              a_a0[i], a_a1[i], a_a2[i], vres)
            if i == 0:
                v_first = v
            z = _wkv(r, wl, k, v, a, a_k_k[i], a_k_a[i], a_r_k[i], a_gn_g[i], a_gn_b[i])
            w_o = a_w_o[i]
        else:
            j = i - n_a
            q, gate = _qproj(h, b_ln_g[j], b_w_in[j])
            z = _attention(q, k_att, v_att)
            w_o = b_w_o[j]
        if i == n_a - 1:
            h, k_att, v_att = _out_layer(z, gate, h, p[i], w_o, ple_w[i], ple_gate_ln_g[i], ple_w_gate[i],
                                         "kv", (kv_ln_g, w_kv))
        elif i == depth - 1:
            h = _out_layer(z, gate, h, p[i], w_o, ple_w[i], ple_gate_ln_g[i], ple_w_gate[i], "final", (final_ln_g,))
        else:
            h = _out_layer(z, gate, h, p[i], w_o, ple_w[i], ple_gate_ln_g[i], ple_w_gate[i])
    return h
```

```python
import functools

import jax
import jax.numpy as jnp
from jax import lax
from jax.experimental import pallas as pl
from jax.experimental.pallas import tpu as pltpu

D_MODEL = 1024
PLE_DIM = 256
NORM_EPS = 1e-6
GN_EPS = 64e-5
RW_HEAD = 64
ATT_HEAD = 128
KV_HEADS = 8
ATT_WIDTH = KV_HEADS * ATT_HEAD
DILATIONS = (1, 4, 16)
N_GROUPS = len(DILATIONS)
Q_WIDTH = N_GROUPS * ATT_WIDTH
BLOCK = 128
LANES = 128
CHUNK = 64
NEG_BIG = -1e30
ACT_DTYPE = jnp.bfloat16

PROJ_ROWS = 256
OUT_ROWS = 512
WKV_ROWS = 512
WKV_STAGGER = 10
ATT_ROWS = 2048
QUARTER = 4 * BLOCK
ATT_UNROLL = 8

_NN = (((1,), (0,)), ((), ()))
_NT = (((1,), (1,)), ((), ()))
_TN = (((0,), (0,)), ((), ()))


def _bf(x):
    return x.astype(jnp.bfloat16)


def _dot(a, b, dims=_NN):
    return lax.dot_general(_bf(a), _bf(b), dims, preferred_element_type=jnp.float32)


def _dot_hilo(a, b_bf16):
    hi = _bf(a)
    lo = _bf(a - hi.astype(jnp.float32))
    return (lax.dot_general(hi, b_bf16, _NN, preferred_element_type=jnp.float32)
            + lax.dot_general(lo, b_bf16, _NN, preferred_element_type=jnp.float32))


def _rms(x, g):
    return x * lax.rsqrt(jnp.mean(x * x, axis=-1, keepdims=True) + NORM_EPS) * g


def _sigmoid(x):
    return 1.0 / (1.0 + jnp.exp(-x))


def _params(vmem_mb, *sem):
    return pltpu.CompilerParams(dimension_semantics=sem, vmem_limit_bytes=vmem_mb * 1024 * 1024)


def _const_spec(shape):
    zeros = (0,) * len(shape)
    return pl.BlockSpec(shape, lambda *_: zeros)


def _rwkv_proj_body(has_vres, *refs):
    if has_vres:
        (h_ref, hprev_ref, vfirst_ref, g_ref, mu_ref, w_ref, w0_ref, w1_ref, w2_ref, a0_ref, a1_ref, a2_ref,
         v0_ref, v1_ref, v2_ref, r_out, wl_out, k_out, v_out, a_out, gate_out) = refs
    else:
        (h_ref, hprev_ref, g_ref, mu_ref, w_ref, w0_ref, w1_ref, w2_ref, a0_ref, a1_ref, a2_ref,
         r_out, wl_out, k_out, v_out, a_out, gate_out) = refs
    g = g_ref[...]
    x = _rms(h_ref[0], g)
    rows = x.shape[0]
    xlast = _rms(hprev_ref[0][7:8, :], g)
    xlast = jnp.where(pl.program_id(1) > 0, xlast, 0.0)
    row = lax.broadcasted_iota(jnp.int32, (rows, 1), 0)
    xprev = jnp.where(row == 0, xlast, pltpu.roll(x, 1, axis=0))
    xx = xprev - x

    def mix(n):
        return _bf(x + xx * mu_ref[n:n + 1, :])

    r_out[0] = jnp.dot(mix(0), w_ref[0], preferred_element_type=jnp.float32).astype(r_out.dtype)
    k_out[0] = jnp.dot(mix(1), w_ref[1], preferred_element_type=jnp.float32).astype(k_out.dtype)
    xv = mix(2)
    v = jnp.dot(xv, w_ref[2], preferred_element_type=jnp.float32)
    if has_vres:
        vmix = _sigmoid(v0_ref[...] + _dot(jnp.dot(xv, v1_ref[...], preferred_element_type=jnp.float32),
                                           v2_ref[...]))
        v = v + (vfirst_ref[0].astype(jnp.float32) - v) * vmix
    v_out[0] = v.astype(v_out.dtype)
    gate_out[0] = jnp.dot(mix(3), w_ref[3], preferred_element_type=jnp.float32).astype(gate_out.dtype)
    lw = jnp.tanh(jnp.dot(mix(4), w1_ref[...], preferred_element_type=jnp.float32))
    z = -(w0_ref[...] + _dot(lw, w2_ref[...]))
    softplus = jnp.maximum(z, 0.0) + jnp.log(1.0 + jnp.exp(-jnp.abs(z)))
    wl_out[0] = -jnp.exp(-softplus - 0.5)
    la = jnp.dot(mix(5), a1_ref[...], preferred_element_type=jnp.float32)
    a_out[0] = _sigmoid(a0_ref[...] + _dot(la, a2_ref[...])).astype(a_out.dtype)


def _rwkv_proj(h, v_first, ln_g, mu, w_rkvg, w0, w1, w2, a0, a1, a2, vres):
    B, S, D = h.shape
    rows = PROJ_ROWS
    has_vres = vres is not None
    tile = pl.BlockSpec((1, rows, D), lambda b, s: (b, s, 0))
    prev = pl.BlockSpec((1, 8, D), lambda b, s: (b, jnp.maximum(s * (rows // 8) - 1, 0), 0))
    row = lambda t: t.reshape(1, -1)
    args = [h, h]
    specs = [tile, prev]
    if has_vres:
        args.append(v_first)
        specs.append(tile)
    weights = [row(ln_g), mu, _bf(w_rkvg), row(w0), _bf(w1), _bf(w2), row(a0), _bf(a1), _bf(a2)]
    if has_vres:
        weights += [row(vres[0]), _bf(vres[1]), _bf(vres[2])]
    args += weights
    specs += [_const_spec(w.shape) for w in weights]
    out = [jax.ShapeDtypeStruct((B, S, D), jnp.float32 if i == 1 else ACT_DTYPE) for i in range(6)]
    return pl.pallas_call(
        functools.partial(_rwkv_proj_body, has_vres),
        grid=(B, S // rows),
        in_specs=specs,
        out_specs=[tile] * 6,
        out_shape=out,
        compiler_params=_params(56, "parallel", "arbitrary"),
        name="rwkv_proj_vres" if has_vres else "rwkv_proj",
    )(*args)


def _wkv_body(r_ref, wl_ref, k_ref, v_ref, a_ref, kk_ref, ka_ref, rk_ref, gng_ref, gnb_ref, out_ref, state_ref):
    C = CHUNK
    n_pairs = r_ref.shape[2] // LANES
    n_chunks = r_ref.shape[1] // C

    @pl.when(pl.program_id(1) == 0)
    def _():
        state_ref[...] = jnp.zeros_like(state_ref)

    lane = lax.broadcasted_iota(jnp.int32, (1, LANES), 1)
    head0 = lane < RW_HEAD
    rowc = lax.broadcasted_iota(jnp.int32, (C, LANES), 0)
    colc = lax.broadcasted_iota(jnp.int32, (C, LANES), 1) % RW_HEAD
    strict = rowc > colc
    incl2 = jnp.concatenate([rowc >= colc, rowc >= colc], axis=1)
    eye_pair = (rowc == colc).astype(jnp.float32)
    rr = lax.broadcasted_iota(jnp.int32, (LANES, LANES), 0)
    cc = lax.broadcasted_iota(jnp.int32, (LANES, LANES), 1)
    same_head = (rr // RW_HEAD) == (cc // RW_HEAD)
    ones_bd = jnp.where(same_head, 1.0, 0.0).astype(jnp.bfloat16)
    eye_full = (rr == cc).astype(jnp.float32)
    tr = lax.broadcasted_iota(jnp.int32, (C, C), 0)
    tc = lax.broadcasted_iota(jnp.int32, (C, C), 1)
    tri = jnp.where(tr >= tc, 1.0, 0.0).astype(jnp.bfloat16)

    def bstack(x):
        xb = _bf(x)
        zero = jnp.zeros_like(xb)
        return jnp.concatenate([jnp.where(head0, xb, zero), jnp.where(head0, zero, xb)], axis=0)

    def head_sums(xs):
        return [jnp.where(head0,
                          jnp.sum(jnp.where(head0, x, 0.0), axis=-1, keepdims=True),
                          jnp.sum(jnp.where(head0, 0.0, x), axis=-1, keepdims=True)) for x in xs]

    def pmm(x_pair, y):
        return lax.dot_general(_bf(x_pair), bstack(y), _NN, preferred_element_type=jnp.float32)

    def prefix_sum(wl):
        w1 = _bf(wl)
        rem = wl - w1.astype(jnp.float32)
        w2 = _bf(rem)
        w3 = _bf(rem - w2.astype(jnp.float32))
        return (jnp.dot(tri, w1, preferred_element_type=jnp.float32)
                + jnp.dot(tri, w2, preferred_element_type=jnp.float32)
                + jnp.dot(tri, w3, preferred_element_type=jnp.float32))

    pairs = range(n_pairs)
    lanes = [slice(p * LANES, (p + 1) * LANES) for p in pairs]
    state = [state_ref[p] for p in pairs]

    def chunk_steps(ci):
        t0 = ci * C
        wl_all = wl_ref[0, t0:t0 + C, :]
        cw_all = prefix_sum(wl_all)
        r = [r_ref[0, t0:t0 + C, ls].astype(jnp.float32) for ls in lanes]
        k = [k_ref[0, t0:t0 + C, ls].astype(jnp.float32) for ls in lanes]
        v = [v_ref[0, t0:t0 + C, ls].astype(jnp.float32) for ls in lanes]
        a = [a_ref[0, t0:t0 + C, ls].astype(jnp.float32) for ls in lanes]
        kk = [k[p] * kk_ref[:, lanes[p]] for p in pairs]
        ss = head_sums([kk[p] * kk[p] for p in pairs])
        yield
        k = [k[p] * (1.0 + (a[p] - 1.0) * ka_ref[:, lanes[p]]) for p in pairs]
        rk = head_sums([r[p] * k[p] * rk_ref[:, lanes[p]] for p in pairs])
        yield
        kk = [kk[p] * lax.rsqrt(jnp.maximum(ss[p], 1e-24)) for p in pairs]
        cw = [cw_all[:, ls] for ls in lanes]
        wc = [cw[p][C - 1:C, :] for p in pairs]
        r_t = [r[p] * jnp.exp(cw[p]) for p in pairs]
        a_t = [-kk[p] * jnp.exp(cw[p] - wl_all[:, lanes[p]]) for p in pairs]
        e_end = [jnp.exp(wc[p] - cw[p]) for p in pairs]
        b_e = [kk[p] * a[p] * e_end[p] for p in pairs]
        k_e = [k[p] * e_end[p] for p in pairs]
        e_up = [jnp.exp(-wc[p]) for p in pairs]
        am = [lax.dot_general(_bf(jnp.concatenate([a_t[p] * e_up[p], r_t[p] * e_up[p]], axis=0)),
                              jnp.concatenate([bstack(b_e[p]), bstack(k_e[p])], axis=0),
                              _NT, preferred_element_type=jnp.float32) for p in pairs]
        yield
        lmat =[jnp.where(strict, am[p][:C, :LANES], 0.0) for p in pairs]
        a_ak = [jnp.where(strict, am[p][:C, LANES:], 0.0) for p in pairs]
        a_r = [_bf(jnp.where(incl2, am[p][C:, :], 0.0)) for p in pairs]
        x = [pmm(a_ak[p], v[p]) for p in pairs]
        yield
        tmat = [eye_pair + lmat[p] for p in pairs]
        pw = lmat
        for _ in range(5):
            pw = [pmm(pw[p], pw[p]) for p in pairs]
            yield
            tmat = [tmat[p] + pmm(tmat[p], pw[p]) for p in pairs]
            yield
        apul = [lax.dot_general(_bf(tmat[p]), jnp.concatenate([bstack(a_t[p]), bstack(x[p])], axis=1),
                                _NN, preferred_element_type=jnp.float32) for p in pairs]
        yield
        vb =[bstack(v[p]) for p in pairs]
        zb = jnp.zeros((LANES, LANES), jnp.bfloat16)
        rpyl = [lax.dot_general(a_r[p],
                                jnp.concatenate([jnp.concatenate([bstack(apul[p][:, :LANES]), zb], axis=0),
                                                 jnp.concatenate([bstack(apul[p][:, LANES:]), vb[p]], axis=0)], axis=1),
                                _NN, preferred_element_type=jnp.float32) for p in pairs]
        zc = jnp.zeros((C, LANES), jnp.bfloat16)
        mg = [lax.dot_general(_bf(jnp.concatenate([b_e[p], k_e[p]], axis=0)),
                              jnp.concatenate([_bf(apul[p]), jnp.concatenate([zc, _bf(v[p])], axis=1)], axis=0),
                              _TN, preferred_element_type=jnp.float32) for p in pairs]
        yield
        r_p = [r_t[p] + rpyl[p][:, :LANES] for p in pairs]
        m_d = [eye_full * jnp.exp(wc[p]) + jnp.where(same_head, mg[p][:, :LANES], 0.0) for p in pairs]
        ys = [lax.dot_general(_bf(jnp.concatenate([r_p[p], m_d[p]], axis=0)), _bf(state[p]), _NN,
                              preferred_element_type=jnp.float32) for p in pairs]
        for p in pairs:
            state[p] = ys[p][C:, :] + jnp.where(same_head, mg[p][:, LANES:], 0.0)
        y = [ys[p][:C, :] + rpyl[p][:, LANES:] for p in pairs]
        inv_n = 1.0 / RW_HEAD
        mean = head_sums(y)
        yield
        yc = [y[p] - mean[p] * inv_n for p in pairs]
        var = head_sums([yc[p] * yc[p] for p in pairs])
        yield
        for p in pairs:
            yn = yc[p] * lax.rsqrt(var[p] * inv_n + GN_EPS) * gng_ref[:, lanes[p]] + gnb_ref[:, lanes[p]]
            out_ref[0, t0:t0 + C, lanes[p]] = yn + rk[p] * v[p]

    steps = [chunk_steps(ci) for ci in range(n_chunks)]
    live = set(range(n_chunks))
    tick = 0
    while live:
        for ci in sorted(live):
            if tick >= ci * WKV_STAGGER:
                try:
                    next(steps[ci])
                except StopIteration:
                    live.discard(ci)
        tick += 1
    for p in pairs:
        state_ref[p] = state[p]


def _wkv(r, wl, k, v, a, k_k, k_a, r_k, gn_g, gn_b):
    B, S, D = r.shape
    rows = WKV_ROWS
    tile = pl.BlockSpec((1, rows, D), lambda b, s: (b, s, 0))
    vec = _const_spec((1, D))
    row = lambda t: t.reshape(1, -1)
    return pl.pallas_call(
        _wkv_body,
        grid=(B, S // rows),
        in_specs=[tile] * 5 + [vec] * 5,
        out_specs=tile,
        out_shape=jax.ShapeDtypeStruct((B, S, D), jnp.float32),
        scratch_shapes=[pltpu.VMEM((D // LANES, LANES, LANES), jnp.float32)],
        compiler_params=_params(48, "parallel", "arbitrary"),
        name="wkv7_chunked",
    )(r, wl, k, v, a, row(k_k), row(k_a), row(r_k), row(gn_g), row(gn_b))


def _out_body(mode, *refs):
    if mode == "kv":
        (z_ref, gate_ref, h_ref, p_ref, wo_ref, pw_ref, pg_ref, pwg_ref, kvg_ref, wkv_ref,
         h_out, k1_out, v1_out, k4_out, v4_out, k16_out, v16_out, kv_scr, kv_scr4) = refs
    elif mode == "final":
        (z_ref, gate_ref, h_ref, p_ref, wo_ref, pw_ref, pg_ref, pwg_ref, fg_ref, h_out) = refs
    else:
        (z_ref, gate_ref, h_ref, p_ref, wo_ref, pw_ref, pg_ref, pwg_ref, h_out) = refs
    gate = gate_ref[0].astype(jnp.float32)
    zg = z_ref[0] * (gate * _sigmoid(gate))
    h = h_ref[0] + _dot(zg, wo_ref[...])
    pgate = _sigmoid(_dot(_rms(h, pg_ref[...]), pwg_ref[...]))
    h = h + _dot(p_ref[0], pw_ref[...]) * pgate
    if mode == "final":
        h_out[0] = _rms(h, fg_ref[...])
    else:
        h_out[0] = h
    if mode == "kv":
        xn = _bf(_rms(h, kvg_ref[...]))
        rows = xn.shape[0]
        heads_tiles = ATT_WIDTH // LANES
        for part, (dst1, dst4, dst16) in enumerate(((k1_out, k4_out, k16_out), (v1_out, v4_out, v16_out))):
            kv = jnp.dot(xn, wkv_ref[:, part * ATT_WIDTH:(part + 1) * ATT_WIDTH], preferred_element_type=jnp.float32)
            dst1[0] = _bf(kv)
            for t in range(heads_tiles):
                ls = slice(t * LANES, (t + 1) * LANES)
                slab = part * heads_tiles + t
                kv_scr[slab] = kv[:, ls]
                for lo in range(4):
                    by4 = kv_scr[slab, pl.ds(lo, rows // 4, stride=4), :]
                    dst4[0, lo, :, ls] = _bf(by4)
                    kv_scr4[slab, lo] = by4
                for lo in range(4):
                    for hi in range(4):
                        dst16[0, lo + 4 * hi, :, ls] = _bf(kv_scr4[slab, lo, pl.ds(hi, rows // 16, stride=4), :])


def _out_layer(z, gate, h, p_i, w_o, ple_w, ple_g, ple_wg, mode="plain", extra=()):
    B, S, D = h.shape
    rows = OUT_ROWS
    tile = pl.BlockSpec((1, rows, D), lambda b, s: (b, s, 0))
    ptile = pl.BlockSpec((1, rows, PLE_DIM), lambda b, s: (b, s, 0))
    weights = [_bf(w_o), _bf(ple_w), ple_g.reshape(1, -1), _bf(ple_wg)]
    if mode == "kv":
        weights += [extra[0].reshape(1, -1), _bf(extra[1])]
    elif mode == "final":
        weights += [extra[0].reshape(1, -1)]
    out_specs = [tile]
    out_shape = [jax.ShapeDtypeStruct((B, S, D), jnp.float32)]
    scratch = []
    if mode == "kv":
        out_specs += [tile, tile]
        out_shape += [jax.ShapeDtypeStruct((B, S, ATT_WIDTH), jnp.bfloat16)] * 2
        for d in DILATIONS[1:]:
            out_specs += [pl.BlockSpec((1, d, rows // d, ATT_WIDTH), lambda b, s: (b, 0, s, 0))] * 2
            out_shape += [jax.ShapeDtypeStruct((B, d, S // d, ATT_WIDTH), jnp.bfloat16)] * 2
        slabs = 2 * ATT_WIDTH // LANES
        scratch = [pltpu.VMEM((slabs, rows, LANES), jnp.float32), pltpu.VMEM((slabs, 4, rows // 4, LANES), jnp.float32)]
    res = pl.pallas_call(
        functools.partial(_out_body, mode),
        grid=(B, S // rows),
        in_specs=[tile, tile, tile, ptile] + [_const_spec(w.shape) for w in weights],
        out_specs=out_specs,
        out_shape=out_shape,
        scratch_shapes=scratch,
        compiler_params=_params(56, "parallel", "arbitrary"),
        name="out_layer_" + mode,
    )(z, gate, h, p_i, *weights)
    return res if mode == "kv" else res[0]


def _qproj_body(h_ref, g_ref, w_ref, q0_out, q1_out, q2_out, gate_out, q_scr, q_scr4):
    xn = _bf(_rms(h_ref[0], g_ref[...]))
    L = BLOCK
    heads_tiles = ATT_WIDTH // LANES
    for g in range(N_GROUPS):
        proj = jnp.dot(xn, w_ref[:, g * ATT_WIDTH:(g + 1) * ATT_WIDTH], preferred_element_type=jnp.float32)
        proj = proj * (ATT_HEAD ** -0.5)
        for t in range(heads_tiles):
            ls = slice(t * LANES, (t + 1) * LANES)
            slab = g * heads_tiles + t
            q_scr[slab] = proj[:, ls]
            for lo in range(4):
                q_scr4[slab, lo] = q_scr[slab, pl.ds(lo, QUARTER // 4, stride=4), :]
            if g == 0:
                for c in range(QUARTER // L):
                    for hi in range(4):
                        for lo in range(0, 4, 2):
                            two = jnp.concatenate(
                                [q_scr4[slab, lo, pl.ds(c * (L // 4) + hi, 8, stride=4), :],
                                 q_scr4[slab, lo + 1, pl.ds(c * (L // 4) + hi, 8, stride=4), :]], axis=0)
                            res = lo + 4 * hi
                            q0_out[0, 0, c, res * 8:(res + 2) * 8, ls] = _bf(two)
            elif g == 1:
                for lo in range(4):
                    for hi in range(4):
                        q1_out[0, 0, 0, lo, hi * 32:(hi + 1) * 32, ls] = _bf(
                            q_scr4[slab, lo, pl.ds(hi, QUARTER // 16, stride=4), :])
            else:
                for lo in range(4):
                    for hi in range(4):
                        q2_out[0, 0, lo + 4 * hi, :, ls] = _bf(q_scr4[slab, lo, pl.ds(hi, QUARTER // 16, stride=4), :])
    gate_out[0] = jnp.dot(xn, w_ref[:, Q_WIDTH:], preferred_element_type=jnp.float32).astype(gate_out.dtype)


def _qproj(h, ln_g, w_in):
    B, S, D = h.shape
    rows = QUARTER
    n_tiles = S // ATT_ROWS
    tile = pl.BlockSpec((1, rows, D), lambda b, s: (b, s, 0))
    q_shape = jax.ShapeDtypeStruct((B, n_tiles, 16, BLOCK, ATT_WIDTH), jnp.bfloat16)
    q0, q1, q2, gate = pl.pallas_call(
        _qproj_body,
        grid=(B, S // rows),
        in_specs=[tile, _const_spec((1, D)), _const_spec(w_in.shape)],
        out_specs=[pl.BlockSpec((1, 1, 4, BLOCK, ATT_WIDTH), lambda b, s: (b, s // 4, s % 4, 0, 0)),
                   pl.BlockSpec((1, 1, 1, 4, BLOCK, ATT_WIDTH), lambda b, s: (b, s // 4, s % 4, 0, 0, 0)),
                   pl.BlockSpec((1, 1, 16, BLOCK // 4, ATT_WIDTH), lambda b, s: (b, s // 4, 0, s % 4, 0)),
                   tile],
        out_shape=[q_shape,
                   jax.ShapeDtypeStruct((B, n_tiles, 4, 4, BLOCK, ATT_WIDTH), jnp.bfloat16),
                   q_shape,
                   jax.ShapeDtypeStruct((B, S, ATT_WIDTH), ACT_DTYPE)],
        scratch_shapes=[pltpu.VMEM((Q_WIDTH // LANES, rows, LANES), jnp.float32),
                        pltpu.VMEM((Q_WIDTH // LANES, 4, rows // 4, LANES), jnp.float32)],
        compiler_params=_params(56, "parallel", "arbitrary"),
        name="att_qproj",
    )(h, ln_g.reshape(1, -1), _bf(w_in))
    return (q0, q1.reshape(q0.shape), q2), gate


def _att_body(q0_ref, q1_ref, q2_ref, k1_ref, v1_ref, k4_ref, v4_ref, k16_ref, v16_ref, out_ref,
              acc_ref, l_ref, m_ref):
    L = BLOCK
    head = pl.program_id(1)
    tile = pl.program_id(2)
    n_units = q0_ref.shape[0]
    slope = jnp.exp2(-(head + 1).astype(jnp.float32))
    rho = lax.broadcasted_iota(jnp.int32, (L, 2 * L), 0)
    kj = lax.broadcasted_iota(jnp.int32, (L, 2 * L), 1)
    ones = jnp.ones((2 * L, ATT_HEAD), jnp.bfloat16)

    def bias_pair(pos):
        delta = L + pos - kj
        regular = jnp.where((delta >= 0) & (delta <= L), -slope * delta.astype(jnp.float32), NEG_BIG)
        delta = pos - kj
        first = jnp.where(delta >= 0, -slope * delta.astype(jnp.float32), NEG_BIG)
        return regular, first

    def attend(q, kcat, vcat, bias):
        s = lax.dot_general(q, kcat, _NT, preferred_element_type=jnp.float32) + bias
        m = jnp.max(s, axis=-1, keepdims=True)
        p = _bf(jnp.exp(s - m))
        v_aug = jnp.concatenate([vcat, ones], axis=1)
        return lax.dot_general(p, v_aug, _NN, preferred_element_type=jnp.float32), m

    def run_group(q_ref, pos, unit_io):
        regular, first = bias_pair(pos)

        def units(i, carry):
            us = [i * ATT_UNROLL + j for j in range(ATT_UNROLL)]
            io = [unit_io(u) for u in us]
            res = [attend(q_ref[u], k, v, jnp.where(is_first, first, regular))
                   for u, (k, v, is_first, _) in zip(us, io)]
            for (pv, m), (_, _, _, store) in zip(res, io):
                store(acc_ref, pv[:, :ATT_HEAD])
                store(l_ref, pv[:, ATT_HEAD:])
                store(m_ref, jnp.broadcast_to(m, (L, ATT_HEAD)))
            return carry

        lax.fori_loop(0, n_units // ATT_UNROLL, units, 0)

    def start_of(block):
        return pl.multiple_of(jnp.maximum(block - 1, 0) * L, L)

    def io_1(u):
        block = tile * n_units + u
        rows = pl.ds(start_of(block), 2 * L)

        def store(ref, val):
            for res in range(16):
                ref[0, res, pl.ds(pl.multiple_of(u * 8, 8), 8), :] = val[res * 8:(res + 1) * 8, :]
        return k1_ref[rows, :], v1_ref[rows, :], block == 0, store

    run_group(q0_ref, (rho % 8) * 16 + rho // 8, io_1)

    def io_4(u):
        c, res4 = u // 4, u % 4
        block = tile * 4 + c
        rows = pl.ds(start_of(block), 2 * L)

        def store(ref, val):
            for j in range(4):
                ref[1, 4 * j + res4, pl.ds(pl.multiple_of(c * 32, 32), 32), :] = val[j * 32:(j + 1) * 32, :]
        return k4_ref[res4, rows, :], v4_ref[res4, rows, :], block == 0, store

    run_group(q1_ref, (rho % 32) * 4 + rho // 32, io_4)

    def io_16(u):
        rows = pl.ds(start_of(tile), 2 * L)

        def store(ref, val):
            ref[2, u] = val
        return k16_ref[u, rows, :], v16_ref[u, rows, :], tile == 0, store

    run_group(q2_ref, rho, io_16)

    m_all = jnp.maximum(jnp.maximum(m_ref[0], m_ref[1]), m_ref[2])
    num = jnp.zeros(m_all.shape, jnp.float32)
    den = jnp.zeros(m_all.shape, jnp.float32)
    for g in range(N_GROUPS):
        e = jnp.exp(m_ref[g] - m_all)
        num = num + e * acc_ref[g]
        den = den + e * l_ref[g]
    merged = num / den
    for res in range(16):
        out_ref[0, pl.ds(res, L, stride=16), :] = merged[res]


def _attention(q, kv):
    B, S, _ = kv[0].shape
    rows = ATT_ROWS
    q_spec = pl.BlockSpec((None, None, 16, BLOCK, ATT_HEAD), lambda b, h, t: (b, t, 0, 0, h))
    kv_specs = [pl.BlockSpec((None, S, ATT_HEAD), lambda b, h, t: (b, 0, h))] * 2
    for d in DILATIONS[1:]:
        kv_specs += [pl.BlockSpec((None, d, S // d, ATT_HEAD), lambda b, h, t: (b, 0, 0, h))] * 2
    scratch = pltpu.VMEM((N_GROUPS, 16, BLOCK, ATT_HEAD), jnp.float32)
    return pl.pallas_call(
        _att_body,
        grid=(B, KV_HEADS, S // rows),
        in_specs=[q_spec] * N_GROUPS + kv_specs,
        out_specs=pl.BlockSpec((1, rows, ATT_HEAD), lambda b, h, t: (b, t, h)),
        out_shape=jax.ShapeDtypeStruct((B, S, ATT_WIDTH), jnp.float32),
        scratch_shapes=[scratch, scratch, scratch],
        compiler_params=_params(48, "parallel", "parallel", "arbitrary"),
        name="dilated_attention",
    )(*q, *kv)


def kernel(x, p, a_ln_g, a_mu, a_w_rkvg, a_w0, a_w1, a_w2, a_a0, a_a1, a_a2, a_v0, a_v1, a_v2, a_k_k, a_k_a, a_r_k, a_gn_g, a_gn_b, a_w_o, kv_ln_g, w_kv, b_ln_g, b_w_in, b_w_o, ple_w, ple_gate_ln_g, ple_w_gate, final_ln_g):
    n_a = a_ln_g.shape[0]
    depth = p.shape[0]
    assert x.shape[1] % ATT_ROWS == 0 and x.shape[2] == D_MODEL
    h = x
    v_first = None
    kv_att = None
    for i in range(depth):
        if i < n_a:
            vres = None if i == 0 else (a_v0[i - 1], a_v1[i - 1], a_v2[i - 1])
            r, wl, k, v, a, gate = _rwkv_proj(h, v_first, a_ln_g[i], a_mu[i], a_w_rkvg[i], a_w0[i], a_w1[i], a_w2[i],
                                              a_a0[i], a_a1[i], a_a2[i], vres)
            if i == 0:
                v_first = v
            z = _wkv(r, wl, k, v, a, a_k_k[i], a_k_a[i], a_r_k[i], a_gn_g[i], a_gn_b[i])
            w_o = a_w_o[i]
        else:
            j = i - n_a
            q, gate = _qproj(h, b_ln_g[j], b_w_in[j])
            z = _attention(q, kv_att)
            w_o = b_w_o[j]
        if i == n_a - 1:
            h, *kv_att = _out_layer(z, gate, h, p[i], w_o, ple_w[i], ple_gate_ln_g[i], ple_w_gate[i],
                                    "kv", (kv_ln_g, w_kv))
        elif i == depth - 1:
            h = _out_layer(z, gate, h, p[i], w_o, ple_w[i], ple_gate_ln_g[i], ple_w_gate[i], "final", (final_ln_g,))
        else:
            h = _out_layer(z, gate, h, p[i], w_o, ple_w[i], ple_gate_ln_g[i], ple_w_gate[i])
    return h
```

```python
import functools

import jax
import jax.numpy as jnp
from jax import lax
from jax.experimental import pallas as pl
from jax.experimental.pallas import tpu as pltpu

D_MODEL = 1024
PLE_DIM = 256
NORM_EPS = 1e-6
GN_EPS = 64e-5
RW_HEAD = 64
ATT_HEAD = 128
KV_HEADS = 8
ATT_WIDTH = KV_HEADS * ATT_HEAD
DILATIONS = (1, 4, 16)
N_GROUPS = len(DILATIONS)
Q_WIDTH = N_GROUPS * ATT_WIDTH
BLOCK = 128
LANES = 128
CHUNK = 64
NEG_BIG = -1e30
ACT_DTYPE = jnp.bfloat16

PROJ_ROWS = 512
OUT_ROWS = 512
WKV_ROWS = 512
WKV_STAGGER = 10
ATT_ROWS = 2048
QUARTER = 4 * BLOCK
ATT_UNROLL = 4

_NN = (((1,), (0,)), ((), ()))
_NT = (((1,), (1,)), ((), ()))
_TN = (((0,), (0,)), ((), ()))


def _bf(x):
    return x.astype(jnp.bfloat16)


def _dot(a, b, dims=_NN):
    return lax.dot_general(_bf(a), _bf(b), dims, preferred_element_type=jnp.float32)


def _dot_hilo(a, b_bf16):
    hi = _bf(a)
    lo = _bf(a - hi.astype(jnp.float32))
    return (lax.dot_general(hi, b_bf16, _NN, preferred_element_type=jnp.float32)
            + lax.dot_general(lo, b_bf16, _NN, preferred_element_type=jnp.float32))


def _rms(x, g):
    return x * lax.rsqrt(jnp.mean(x * x, axis=-1, keepdims=True) + NORM_EPS) * g


def _sigmoid(x):
    return 1.0 / (1.0 + jnp.exp(-x))


def _params(vmem_mb, *sem):
    return pltpu.CompilerParams(dimension_semantics=sem, vmem_limit_bytes=vmem_mb * 1024 * 1024)


def _const_spec(shape):
    zeros = (0,) * len(shape)
    return pl.BlockSpec(shape, lambda *_: zeros)


def _rwkv_proj_body(has_vres, *refs):
    if has_vres:
        (h_ref, hprev_ref, vfirst_ref, g_ref, mu_ref, w_ref, w0_ref, w1_ref, w2_ref, a0_ref, a1_ref, a2_ref,
         v0_ref, v1_ref, v2_ref, r_out, wl_out, k_out, v_out, a_out, gate_out) = refs
    else:
        (h_ref, hprev_ref, g_ref, mu_ref, w_ref, w0_ref, w1_ref, w2_ref, a0_ref, a1_ref, a2_ref,
         r_out, wl_out, k_out, v_out, a_out, gate_out) = refs
    g = g_ref[...]
    x = _rms(h_ref[0], g)
    rows = x.shape[0]
    xlast = _rms(hprev_ref[0][7:8, :], g)
    xlast = jnp.where(pl.program_id(1) > 0, xlast, 0.0)
    row = lax.broadcasted_iota(jnp.int32, (rows, 1), 0)
    xprev = jnp.where(row == 0, xlast, pltpu.roll(x, 1, axis=0))
    xx = xprev - x

    def mix(n):
        return _bf(x + xx * mu_ref[n:n + 1, :])

    r_out[0] = jnp.dot(mix(0), w_ref[0], preferred_element_type=jnp.float32).astype(r_out.dtype)
    k_out[0] = jnp.dot(mix(1), w_ref[1], preferred_element_type=jnp.float32).astype(k_out.dtype)
    xv = mix(2)
    v = jnp.dot(xv, w_ref[2], preferred_element_type=jnp.float32)
    if has_vres:
        vmix = _sigmoid(v0_ref[...] + _dot(jnp.dot(xv, v1_ref[...], preferred_element_type=jnp.float32),
                                           v2_ref[...]))
        v = v + (vfirst_ref[0].astype(jnp.float32) - v) * vmix
    v_out[0] = v.astype(v_out.dtype)
    gate_out[0] = jnp.dot(mix(3), w_ref[3], preferred_element_type=jnp.float32).astype(gate_out.dtype)
    lw = jnp.tanh(jnp.dot(mix(4), w1_ref[...], preferred_element_type=jnp.float32))
    z = -(w0_ref[...] + _dot(lw, w2_ref[...]))
    softplus = jnp.maximum(z, 0.0) + jnp.log(1.0 + jnp.exp(-jnp.abs(z)))
    wl_out[0] = -jnp.exp(-softplus - 0.5)
    la = jnp.dot(mix(5), a1_ref[...], preferred_element_type=jnp.float32)
    a_out[0] = _sigmoid(a0_ref[...] + _dot(la, a2_ref[...])).astype(a_out.dtype)


def _rwkv_proj(h, v_first, ln_g, mu, w_rkvg, w0, w1, w2, a0, a1, a2, vres):
    B, S, D = h.shape
    rows = PROJ_ROWS
    has_vres = vres is not None
    tile = pl.BlockSpec((1, rows, D), lambda b, s: (b, s, 0))
    prev = pl.BlockSpec((1, 8, D), lambda b, s: (b, jnp.maximum(s * (rows // 8) - 1, 0), 0))
    row = lambda t: t.reshape(1, -1)
    args = [h, h]
    specs = [tile, prev]
    if has_vres:
        args.append(v_first)
        specs.append(tile)
    weights = [row(ln_g), mu, _bf(w_rkvg), row(w0), _bf(w1), _bf(w2), row(a0), _bf(a1), _bf(a2)]
    if has_vres:
        weights += [row(vres[0]), _bf(vres[1]), _bf(vres[2])]
    args += weights
    specs += [_const_spec(w.shape) for w in weights]
    out = [jax.ShapeDtypeStruct((B, S, D), jnp.float32 if i == 1 else ACT_DTYPE) for i in range(6)]
    return pl.pallas_call(
        functools.partial(_rwkv_proj_body, has_vres),
        grid=(B, S // rows),
        in_specs=specs,
        out_specs=[tile] * 6,
        out_shape=out,
        compiler_params=_params(56, "parallel", "arbitrary"),
        name="rwkv_proj_vres" if has_vres else "rwkv_proj",
    )(*args)


def _wkv_body(r_ref, wl_ref, k_ref, v_ref, a_ref, kk_ref, ka_ref, rk_ref, gng_ref, gnb_ref, out_ref, state_ref):
    C = CHUNK
    n_pairs = r_ref.shape[2] // LANES
    n_chunks = r_ref.shape[1] // C

    @pl.when(pl.program_id(1) == 0)
    def _():
        state_ref[...] = jnp.zeros_like(state_ref)

    lane = lax.broadcasted_iota(jnp.int32, (1, LANES), 1)
    head0 = lane < RW_HEAD
    rowc = lax.broadcasted_iota(jnp.int32, (C, LANES), 0)
    colc = lax.broadcasted_iota(jnp.int32, (C, LANES), 1) % RW_HEAD
    strict = rowc > colc
    incl2 = jnp.concatenate([rowc >= colc, rowc >= colc], axis=1)
    eye_pair = (rowc == colc).astype(jnp.float32)
    rr = lax.broadcasted_iota(jnp.int32, (LANES, LANES), 0)
    cc = lax.broadcasted_iota(jnp.int32, (LANES, LANES), 1)
    same_head = (rr // RW_HEAD) == (cc // RW_HEAD)
    ones_bd = jnp.where(same_head, 1.0, 0.0).astype(jnp.bfloat16)
    eye_full = (rr == cc).astype(jnp.float32)
    tr = lax.broadcasted_iota(jnp.int32, (C, C), 0)
    tc = lax.broadcasted_iota(jnp.int32, (C, C), 1)
    tri = jnp.where(tr >= tc, 1.0, 0.0).astype(jnp.bfloat16)

    def bstack(x):
        xb = _bf(x)
        zero = jnp.zeros_like(xb)
        return jnp.concatenate([jnp.where(head0, xb, zero), jnp.where(head0, zero, xb)], axis=0)

    def head_sums(xs):
        return [jnp.where(head0,
                          jnp.sum(jnp.where(head0, x, 0.0), axis=-1, keepdims=True),
                          jnp.sum(jnp.where(head0, 0.0, x), axis=-1, keepdims=True)) for x in xs]

    def pmm(x_pair, y):
        return lax.dot_general(_bf(x_pair), bstack(y), _NN, preferred_element_type=jnp.float32)

    def prefix_sum(wl):
        w1 = _bf(wl)
        rem = wl - w1.astype(jnp.float32)
        w2 = _bf(rem)
        w3 = _bf(rem - w2.astype(jnp.float32))
        return (jnp.dot(tri, w1, preferred_element_type=jnp.float32)
                + jnp.dot(tri, w2, preferred_element_type=jnp.float32)
                + jnp.dot(tri, w3, preferred_element_type=jnp.float32))

    pairs = range(n_pairs)
    lanes = [slice(p * LANES, (p + 1) * LANES) for p in pairs]
    state = [state_ref[p] for p in pairs]

    def chunk_steps(ci):
        t0 = ci * C
        wl_all = wl_ref[0, t0:t0 + C, :]
        cw_all = prefix_sum(wl_all)
        r = [r_ref[0, t0:t0 + C, ls].astype(jnp.float32) for ls in lanes]
        k = [k_ref[0, t0:t0 + C, ls].astype(jnp.float32) for ls in lanes]
        v = [v_ref[0, t0:t0 + C, ls].astype(jnp.float32) for ls in lanes]
        a = [a_ref[0, t0:t0 + C, ls].astype(jnp.float32) for ls in lanes]
        kk = [k[p] * kk_ref[:, lanes[p]] for p in pairs]
        ss = head_sums([kk[p] * kk[p] for p in pairs])
        yield
        k = [k[p] * (1.0 + (a[p] - 1.0) * ka_ref[:, lanes[p]]) for p in pairs]
        rk = head_sums([r[p] * k[p] * rk_ref[:, lanes[p]] for p in pairs])
        yield
        kk = [kk[p] * lax.rsqrt(jnp.maximum(ss[p], 1e-24)) for p in pairs]
        cw = [cw_all[:, ls] for ls in lanes]
        wc = [cw[p][C - 1:C, :] for p in pairs]
        r_t = [r[p] * jnp.exp(cw[p]) for p in pairs]
        a_t = [-kk[p] * jnp.exp(cw[p] - wl_all[:, lanes[p]]) for p in pairs]
        e_end = [jnp.exp(wc[p] - cw[p]) for p in pairs]
        b_e = [kk[p] * a[p] * e_end[p] for p in pairs]
        k_e = [k[p] * e_end[p] for p in pairs]
        e_up = [jnp.exp(-wc[p]) for p in pairs]
        am = [lax.dot_general(_bf(jnp.concatenate([a_t[p] * e_up[p], r_t[p] * e_up[p]], axis=0)),
                              jnp.concatenate([bstack(b_e[p]), bstack(k_e[p])], axis=0),
                              _NT, preferred_element_type=jnp.float32) for p in pairs]
        yield
        lmat =[jnp.where(strict, am[p][:C, :LANES], 0.0) for p in pairs]
        a_ak = [jnp.where(strict, am[p][:C, LANES:], 0.0) for p in pairs]
        a_r = [_bf(jnp.where(incl2, am[p][C:, :], 0.0)) for p in pairs]
        x = [pmm(a_ak[p], v[p]) for p in pairs]
        yield
        tmat = [eye_pair + lmat[p] for p in pairs]
        pw = lmat
        for _ in range(5):
            pw = [pmm(pw[p], pw[p]) for p in pairs]
            yield
            tmat = [tmat[p] + pmm(tmat[p], pw[p]) for p in pairs]
            yield
        apul = [lax.dot_general(_bf(tmat[p]), jnp.concatenate([bstack(a_t[p]), bstack(x[p])], axis=1),
                                _NN, preferred_element_type=jnp.float32) for p in pairs]
        yield
        vb =[bstack(v[p]) for p in pairs]
        zb = jnp.zeros((LANES, LANES), jnp.bfloat16)
        rpyl = [lax.dot_general(a_r[p],
                                jnp.concatenate([jnp.concatenate([bstack(apul[p][:, :LANES]), zb], axis=0),
                                                 jnp.concatenate([bstack(apul[p][:, LANES:]), vb[p]], axis=0)], axis=1),
                                _NN, preferred_element_type=jnp.float32) for p in pairs]
        zc = jnp.zeros((C, LANES), jnp.bfloat16)
        mg = [lax.dot_general(_bf(jnp.concatenate([b_e[p], k_e[p]], axis=0)),
                              jnp.concatenate([_bf(apul[p]), jnp.concatenate([zc, _bf(v[p])], axis=1)], axis=0),
                              _TN, preferred_element_type=jnp.float32) for p in pairs]
        yield
        r_p = [r_t[p] + rpyl[p][:, :LANES] for p in pairs]
        m_d = [eye_full * jnp.exp(wc[p]) + jnp.where(same_head, mg[p][:, :LANES], 0.0) for p in pairs]
        ys = [lax.dot_general(_bf(jnp.concatenate([r_p[p], m_d[p]], axis=0)), _bf(state[p]), _NN,
                              preferred_element_type=jnp.float32) for p in pairs]
        for p in pairs:
            state[p] = ys[p][C:, :] + jnp.where(same_head, mg[p][:, LANES:], 0.0)
        y = [ys[p][:C, :] + rpyl[p][:, LANES:] for p in pairs]
        inv_n = 1.0 / RW_HEAD
        mean = head_sums(y)
        yield
        yc = [y[p] - mean[p] * inv_n for p in pairs]
        var = head_sums([yc[p] * yc[p] for p in pairs])
        yield
        for p in pairs:
            yn = yc[p] * lax.rsqrt(var[p] * inv_n + GN_EPS) * gng_ref[:, lanes[p]] + gnb_ref[:, lanes[p]]
            out_ref[0, t0:t0 + C, lanes[p]] = yn + rk[p] * v[p]

    steps = [chunk_steps(ci) for ci in range(n_chunks)]
    live = set(range(n_chunks))
    tick = 0
    while live:
        for ci in sorted(live):
            if tick >= ci * WKV_STAGGER:
                try:
                    next(steps[ci])
                except StopIteration:
                    live.discard(ci)
        tick += 1
    for p in pairs:
        state_ref[p] = state[p]


def _wkv(r, wl, k, v, a, k_k, k_a, r_k, gn_g, gn_b):
    B, S, D = r.shape
    rows = WKV_ROWS
    tile = pl.BlockSpec((1, rows, D), lambda b, s: (b, s, 0))
    vec = _const_spec((1, D))
    row = lambda t: t.reshape(1, -1)
    return pl.pallas_call(
        _wkv_body,
        grid=(B, S // rows),
        in_specs=[tile] * 5 + [vec] * 5,
        out_specs=tile,
        out_shape=jax.ShapeDtypeStruct((B, S, D), jnp.float32),
        scratch_shapes=[pltpu.VMEM((D // LANES, LANES, LANES), jnp.float32)],
        compiler_params=_params(48, "parallel", "arbitrary"),
        name="wkv7_chunked",
    )(r, wl, k, v, a, row(k_k), row(k_a), row(r_k), row(gn_g), row(gn_b))


def _out_body(mode, *refs):
    if mode == "kv":
        (z_ref, gate_ref, h_ref, p_ref, wo_ref, pw_ref, pg_ref, pwg_ref, kvg_ref, wkv_ref,
         h_out, k1_out, v1_out, k4_out, v4_out, k16_out, v16_out, kv_scr, kv_scr4) = refs
    elif mode == "final":
        (z_ref, gate_ref, h_ref, p_ref, wo_ref, pw_ref, pg_ref, pwg_ref, fg_ref, h_out) = refs
    else:
        (z_ref, gate_ref, h_ref, p_ref, wo_ref, pw_ref, pg_ref, pwg_ref, h_out) = refs
    gate = gate_ref[0].astype(jnp.float32)
    zg = z_ref[0] * (gate * _sigmoid(gate))
    h = h_ref[0] + _dot(zg, wo_ref[...])
    pgate = _sigmoid(_dot(_rms(h, pg_ref[...]), pwg_ref[...]))
    h = h + _dot(p_ref[0], pw_ref[...]) * pgate
    if mode == "final":
        h_out[0] = _rms(h, fg_ref[...])
    else:
        h_out[0] = h
    if mode == "kv":
        xn = _bf(_rms(h, kvg_ref[...]))
        rows = xn.shape[0]
        heads_tiles = ATT_WIDTH // LANES
        for part, (dst1, dst4, dst16) in enumerate(((k1_out, k4_out, k16_out), (v1_out, v4_out, v16_out))):
            kv = jnp.dot(xn, wkv_ref[:, part * ATT_WIDTH:(part + 1) * ATT_WIDTH], preferred_element_type=jnp.float32)
            dst1[0] = _bf(kv)
            for t in range(heads_tiles):
                ls = slice(t * LANES, (t + 1) * LANES)
                slab = part * heads_tiles + t
                kv_scr[slab] = kv[:, ls]
                for lo in range(4):
                    by4 = kv_scr[slab, pl.ds(lo, rows // 4, stride=4), :]
                    dst4[0, lo, :, ls] = _bf(by4)
                    kv_scr4[slab, lo] = by4
                for lo in range(4):
                    for hi in range(4):
                        dst16[0, lo + 4 * hi, :, ls] = _bf(kv_scr4[slab, lo, pl.ds(hi, rows // 16, stride=4), :])


def _out_layer(z, gate, h, p, layer, w_o, ple_w, ple_g, ple_wg, mode="plain", extra=()):
    B, S, D = h.shape
    rows = OUT_ROWS
    tile = pl.BlockSpec((1, rows, D), lambda b, s: (b, s, 0))
    ptile = pl.BlockSpec((None, 1, rows, PLE_DIM), lambda b, s: (layer, b, s, 0))
    weights = [_bf(w_o), _bf(ple_w), ple_g.reshape(1, -1), _bf(ple_wg)]
    if mode == "kv":
        weights += [extra[0].reshape(1, -1), _bf(extra[1])]
    elif mode == "final":
        weights += [extra[0].reshape(1, -1)]
    out_specs = [tile]
    out_shape = [jax.ShapeDtypeStruct((B, S, D), jnp.float32)]
    scratch = []
    if mode == "kv":
        out_specs += [tile, tile]
        out_shape += [jax.ShapeDtypeStruct((B, S, ATT_WIDTH), jnp.bfloat16)] * 2
        for d in DILATIONS[1:]:
            out_specs += [pl.BlockSpec((1, d, rows // d, ATT_WIDTH), lambda b, s: (b, 0, s, 0))] * 2
            out_shape += [jax.ShapeDtypeStruct((B, d, S // d, ATT_WIDTH), jnp.bfloat16)] * 2
        slabs = 2 * ATT_WIDTH // LANES
        scratch = [pltpu.VMEM((slabs, rows, LANES), jnp.float32), pltpu.VMEM((slabs, 4, rows // 4, LANES), jnp.float32)]
    res = pl.pallas_call(
        functools.partial(_out_body, mode),
        grid=(B, S // rows),
        in_specs=[tile, tile, tile, ptile] + [_const_spec(w.shape) for w in weights],
        out_specs=out_specs,
        out_shape=out_shape,
        scratch_shapes=scratch,
        compiler_params=_params(56, "parallel", "arbitrary"),
        name="out_layer_" + mode,
    )(z, gate, h, p, *weights)
    return res if mode == "kv" else res[0]


def _qproj_body(h_ref, g_ref, w_ref, q0_out, q1_out, q2_out, gate_out, q_scr, q_scr4):
    xn = _bf(_rms(h_ref[0], g_ref[...]))
    L = BLOCK
    heads_tiles = ATT_WIDTH // LANES
    for g in range(N_GROUPS):
        proj = jnp.dot(xn, w_ref[:, g * ATT_WIDTH:(g + 1) * ATT_WIDTH], preferred_element_type=jnp.float32)
        proj = proj * (ATT_HEAD ** -0.5)
        for t in range(heads_tiles):
            ls = slice(t * LANES, (t + 1) * LANES)
            slab = g * heads_tiles + t
            q_scr[slab] = proj[:, ls]
            for lo in range(4):
                q_scr4[slab, lo] = q_scr[slab, pl.ds(lo, QUARTER // 4, stride=4), :]
            if g == 0:
                for c in range(QUARTER // L):
                    for hi in range(4):
                        for lo in range(0, 4, 2):
                            two = jnp.concatenate(
                                [q_scr4[slab, lo, pl.ds(c * (L // 4) + hi, 8, stride=4), :],
                                 q_scr4[slab, lo + 1, pl.ds(c * (L // 4) + hi, 8, stride=4), :]], axis=0)
                            res = lo + 4 * hi
                            q0_out[0, 0, c, res * 8:(res + 2) * 8, ls] = _bf(two)
            elif g == 1:
                for lo in range(4):
                    for hi in range(4):
                        q1_out[0, 0, 0, lo, hi * 32:(hi + 1) * 32, ls] = _bf(
                            q_scr4[slab, lo, pl.ds(hi, QUARTER // 16, stride=4), :])
            else:
                for lo in range(4):
                    for hi in range(4):
                        q2_out[0, 0, lo + 4 * hi, :, ls] = _bf(q_scr4[slab, lo, pl.ds(hi, QUARTER // 16, stride=4), :])
    gate_out[0] = jnp.dot(xn, w_ref[:, Q_WIDTH:], preferred_element_type=jnp.float32).astype(gate_out.dtype)


def _qproj(h, ln_g, w_in):
    B, S, D = h.shape
    rows = QUARTER
    n_tiles = S // ATT_ROWS
    tile = pl.BlockSpec((1, rows, D), lambda b, s: (b, s, 0))
    q_shape = jax.ShapeDtypeStruct((B, n_tiles, 16, BLOCK, ATT_WIDTH), jnp.bfloat16)
    q0, q1, q2, gate = pl.pallas_call(
        _qproj_body,
        grid=(B, S // rows),
        in_specs=[tile, _const_spec((1, D)), _const_spec(w_in.shape)],
        out_specs=[pl.BlockSpec((1, 1, 4, BLOCK, ATT_WIDTH), lambda b, s: (b, s // 4, s % 4, 0, 0)),
                   pl.BlockSpec((1, 1, 1, 4, BLOCK, ATT_WIDTH), lambda b, s: (b, s // 4, s % 4, 0, 0, 0)),
                   pl.BlockSpec((1, 1, 16, BLOCK // 4, ATT_WIDTH), lambda b, s: (b, s // 4, 0, s % 4, 0)),
                   tile],
        out_shape=[q_shape,
                   jax.ShapeDtypeStruct((B, n_tiles, 4, 4, BLOCK, ATT_WIDTH), jnp.bfloat16),
                   q_shape,
                   jax.ShapeDtypeStruct((B, S, ATT_WIDTH), ACT_DTYPE)],
        scratch_shapes=[pltpu.VMEM((Q_WIDTH // LANES, rows, LANES), jnp.float32),
                        pltpu.VMEM((Q_WIDTH // LANES, 4, rows // 4, LANES), jnp.float32)],
        compiler_params=_params(56, "parallel", "arbitrary"),
        name="att_qproj",
    )(h, ln_g.reshape(1, -1), _bf(w_in))
    return (q0, q1.reshape(q0.shape), q2), gate


def _att_body(q0_ref, q1_ref, q2_ref, k1_ref, v1_ref, k4_ref, v4_ref, k16_ref, v16_ref, out_ref,
              acc_ref, l_ref, m_ref):
    L = BLOCK
    head = pl.program_id(1)
    tile = pl.program_id(2)
    n_units = q0_ref.shape[0]
    slope = jnp.exp2(-(head + 1).astype(jnp.float32))
    rho = lax.broadcasted_iota(jnp.int32, (L, 2 * L), 0)
    kj = lax.broadcasted_iota(jnp.int32, (L, 2 * L), 1)
    ones = jnp.ones((2 * L, ATT_HEAD), jnp.bfloat16)

    def bias_pair(pos):
        delta = L + pos - kj
        regular = jnp.where((delta >= 0) & (delta <= L), -slope * delta.astype(jnp.float32), NEG_BIG)
        delta = pos - kj
        first = jnp.where(delta >= 0, -slope * delta.astype(jnp.float32), NEG_BIG)
        return regular, first

    def batch_steps(q_ref, us, unit_io, regular, first):
        io = [unit_io(u) for u in us]
        s = [lax.dot_general(q_ref[u], k, _NT, preferred_element_type=jnp.float32)
             + jnp.where(is_first, first, regular) for u, (k, _, is_first, _) in zip(us, io)]
        yield
        m = [jnp.max(s_, axis=-1, keepdims=True) for s_ in s]
        p = [_bf(jnp.exp(s_ - m_)) for s_, m_ in zip(s, m)]
        yield
        pv = [lax.dot_general(p_, jnp.concatenate([v, ones], axis=1), _NN, preferred_element_type=jnp.float32)
              for p_, (_, v, _, _) in zip(p, io)]
        yield
        for pv_, m_, (_, _, _, store) in zip(pv, m, io):
            store(acc_ref, pv_[:, :ATT_HEAD])
            store(l_ref, pv_[:, ATT_HEAD:])
            store(m_ref, jnp.broadcast_to(m_, (L, ATT_HEAD)))

    def start_of(block):
        return pl.multiple_of(jnp.maximum(block - 1, 0) * L, L)

    def io_1(u):
        block = tile * n_units + u
        rows = pl.ds(start_of(block), 2 * L)

        def store(ref, val):
            for res in range(16):
                ref[0, res, u * 8:(u + 1) * 8, :] = val[res * 8:(res + 1) * 8, :]
        return k1_ref[rows, :], v1_ref[rows, :], block == 0, store

    def io_4(u):
        c, res4 = u // 4, u % 4
        block = tile * 4 + c
        rows = pl.ds(start_of(block), 2 * L)

        def store(ref, val):
            for j in range(4):
                ref[1, 4 * j + res4, c * 32:(c + 1) * 32, :] = val[j * 32:(j + 1) * 32, :]
        return k4_ref[res4, rows, :], v4_ref[res4, rows, :], block == 0, store

    def io_16(u):
        rows = pl.ds(start_of(tile), 2 * L)

        def store(ref, val):
            ref[2, u] = val
        return k16_ref[u, rows, :], v16_ref[u, rows, :], tile == 0, store

    groups = ((q0_ref, (rho % 8) * 16 + rho // 8, io_1),
              (q1_ref, (rho % 32) * 4 + rho // 32, io_4),
              (q2_ref, rho, io_16))
    batches = []
    for q_ref, pos, unit_io in groups:
        regular, first = bias_pair(pos)
        for i in range(n_units // ATT_UNROLL):
            us = range(i * ATT_UNROLL, (i + 1) * ATT_UNROLL)
            batches.append(batch_steps(q_ref, us, unit_io, regular, first))
    live = list(range(len(batches)))
    tick = 0
    while live:
        for i in list(live):
            if tick >= i:
                try:
                    next(batches[i])
                except StopIteration:
                    live.remove(i)
        tick += 1

    m_all = jnp.maximum(jnp.maximum(m_ref[0], m_ref[1]), m_ref[2])
    num = jnp.zeros(m_all.shape, jnp.float32)
    den = jnp.zeros(m_all.shape, jnp.float32)
    for g in range(N_GROUPS):
        e = jnp.exp(m_ref[g] - m_all)
        num = num + e * acc_ref[g]
        den = den + e * l_ref[g]
    merged = num / den
    for res in range(16):
        out_ref[0, pl.ds(res, L, stride=16), :] = merged[res]


def _attention(q, kv):
    B, S, _ = kv[0].shape
    rows = ATT_ROWS
    q_spec = pl.BlockSpec((None, None, 16, BLOCK, ATT_HEAD), lambda b, h, t: (b, t, 0, 0, h))
    kv_specs = [pl.BlockSpec((None, S, ATT_HEAD), lambda b, h, t: (b, 0, h))] * 2
    for d in DILATIONS[1:]:
        kv_specs += [pl.BlockSpec((None, d, S // d, ATT_HEAD), lambda b, h, t: (b, 0, 0, h))] * 2
    scratch = pltpu.VMEM((N_GROUPS, 16, BLOCK, ATT_HEAD), jnp.float32)
    return pl.pallas_call(
        _att_body,
        grid=(B, KV_HEADS, S // rows),
        in_specs=[q_spec] * N_GROUPS + kv_specs,
        out_specs=pl.BlockSpec((1, rows, ATT_HEAD), lambda b, h, t: (b, t, h)),
        out_shape=jax.ShapeDtypeStruct((B, S, ATT_WIDTH), jnp.float32),
        scratch_shapes=[scratch, scratch, scratch],
        compiler_params=_params(48, "parallel", "parallel", "arbitrary"),
        name="dilated_attention",
    )(*q, *kv)


def kernel(x, p, a_ln_g, a_mu, a_w_rkvg, a_w0, a_w1, a_w2, a_a0, a_a1, a_a2, a_v0, a_v1, a_v2, a_k_k, a_k_a, a_r_k, a_gn_g, a_gn_b, a_w_o, kv_ln_g, w_kv, b_ln_g, b_w_in, b_w_o, ple_w, ple_gate_ln_g, ple_w_gate, final_ln_g):
    n_a = a_ln_g.shape[0]
    depth = p.shape[0]
    assert x.shape[1] % ATT_ROWS == 0 and x.shape[2] == D_MODEL
    h = x
    v_first = None
    kv_att = None
    for i in range(depth):
        if i < n_a:
            vres = None if i == 0 else (a_v0[i - 1], a_v1[i - 1], a_v2[i - 1])
            r, wl, k, v, a, gate = _rwkv_proj(h, v_first, a_ln_g[i], a_mu[i], a_w_rkvg[i], a_w0[i], a_w1[i], a_w2[i],
                                              a_a0[i], a_a1[i], a_a2[i], vres)
            if i == 0:
                v_first = v
            z = _wkv(r, wl, k, v, a, a_k_k[i], a_k_a[i], a_r_k[i], a_gn_g[i], a_gn_b[i])
            w_o = a_w_o[i]
        else:
            j = i - n_a
            q, gate = _qproj(h, b_ln_g[j], b_w_in[j])
            z = _attention(q, kv_att)
            w_o = b_w_o[j]
        if i == n_a - 1:
            h, *kv_att = _out_layer(z, gate, h, p, i, w_o, ple_w[i], ple_gate_ln_g[i], ple_w_gate[i],
                                    "kv", (kv_ln_g, w_kv))
        elif i == depth - 1:
            h = _out_layer(z, gate, h, p, i, w_o, ple_w[i], ple_gate_ln_g[i], ple_w_gate[i], "final", (final_ln_g,))
        else:
            h = _out_layer(z, gate, h, p, i, w_o, ple_w[i], ple_gate_ln_g[i], ple_w_gate[i])
    return h
```

```python
import functools

import jax
import jax.numpy as jnp
from jax import lax
from jax.experimental import pallas as pl
from jax.experimental.pallas import tpu as pltpu

D_MODEL = 1024
PLE_DIM = 256
NORM_EPS = 1e-6
GN_EPS = 64e-5
RW_HEAD = 64
ATT_HEAD = 128
KV_HEADS = 8
ATT_WIDTH = KV_HEADS * ATT_HEAD
DILATIONS = (1, 4, 16)
N_GROUPS = len(DILATIONS)
Q_WIDTH = N_GROUPS * ATT_WIDTH
BLOCK = 128
LANES = 128
CHUNK = 64
NEG_BIG = -1e30
LOG2E = 1.4426950408889634
ACT_DTYPE = jnp.bfloat16

PROJ_ROWS = 512
OUT_ROWS = 512
WKV_ROWS = 512
WKV_STAGGER = 3
ATT_ROWS = 2048
QUARTER = 4 * BLOCK
ATT_UNROLL = 4

_NN = (((1,), (0,)), ((), ()))
_NT = (((1,), (1,)), ((), ()))
_TN = (((0,), (0,)), ((), ()))


def _bf(x):
    return x.astype(jnp.bfloat16)


def _dot(a, b, dims=_NN):
    return lax.dot_general(_bf(a), _bf(b), dims, preferred_element_type=jnp.float32)


def _dot_hilo(a, b_bf16):
    hi = _bf(a)
    lo = _bf(a - hi.astype(jnp.float32))
    return (lax.dot_general(hi, b_bf16, _NN, preferred_element_type=jnp.float32)
            + lax.dot_general(lo, b_bf16, _NN, preferred_element_type=jnp.float32))


def _rms(x, g):
    return x * lax.rsqrt(jnp.mean(x * x, axis=-1, keepdims=True) + NORM_EPS) * g


def _sigmoid(x):
    return 1.0 / (1.0 + jnp.exp(-x))


def _params(vmem_mb, *sem):
    return pltpu.CompilerParams(dimension_semantics=sem, vmem_limit_bytes=vmem_mb * 1024 * 1024)


def _const_spec(shape):
    zeros = (0,) * len(shape)
    return pl.BlockSpec(shape, lambda *_: zeros)


def _rwkv_proj_body(has_vres, *refs):
    if has_vres:
        (h_ref, hprev_ref, vfirst_ref, g_ref, mu_ref, w_ref, w0_ref, w1_ref, w2_ref, a0_ref, a1_ref, a2_ref,
         v0_ref, v1_ref, v2_ref, r_out, wl_out, k_out, v_out, a_out, gate_out) = refs
    else:
        (h_ref, hprev_ref, g_ref, mu_ref, w_ref, w0_ref, w1_ref, w2_ref, a0_ref, a1_ref, a2_ref,
         r_out, wl_out, k_out, v_out, a_out, gate_out) = refs
    g = g_ref[...]
    x = _rms(h_ref[0], g)
    rows = x.shape[0]
    xlast = _rms(hprev_ref[0][7:8, :], g)
    xlast = jnp.where(pl.program_id(1) > 0, xlast, 0.0)
    row = lax.broadcasted_iota(jnp.int32, (rows, 1), 0)
    xprev = jnp.where(row == 0, xlast, pltpu.roll(x, 1, axis=0))
    xx = xprev - x

    def mix(n):
        return _bf(x + xx * mu_ref[n:n + 1, :])

    r_out[0] = jnp.dot(mix(0), w_ref[0], preferred_element_type=jnp.float32).astype(r_out.dtype)
    k_out[0] = jnp.dot(mix(1), w_ref[1], preferred_element_type=jnp.float32).astype(k_out.dtype)
    xv = mix(2)
    v = jnp.dot(xv, w_ref[2], preferred_element_type=jnp.float32)
    if has_vres:
        vmix = _sigmoid(v0_ref[...] + _dot(jnp.dot(xv, v1_ref[...], preferred_element_type=jnp.float32),
                                           v2_ref[...]))
        v = v + (vfirst_ref[0].astype(jnp.float32) - v) * vmix
    v_out[0] = v.astype(v_out.dtype)
    gate_out[0] = jnp.dot(mix(3), w_ref[3], preferred_element_type=jnp.float32).astype(gate_out.dtype)
    lw = jnp.tanh(jnp.dot(mix(4), w1_ref[...], preferred_element_type=jnp.float32))
    z = -(w0_ref[...] + _dot(lw, w2_ref[...]))
    softplus = jnp.maximum(z, 0.0) + jnp.log(1.0 + jnp.exp(-jnp.abs(z)))
    wl_out[0] = -jnp.exp(-softplus - 0.5)
    la = jnp.dot(mix(5), a1_ref[...], preferred_element_type=jnp.float32)
    a_out[0] = _sigmoid(a0_ref[...] + _dot(la, a2_ref[...])).astype(a_out.dtype)


def _rwkv_proj(h, v_first, ln_g, mu, w_rkvg, w0, w1, w2, a0, a1, a2, vres):
    B, S, D = h.shape
    rows = PROJ_ROWS
    has_vres = vres is not None
    tile = pl.BlockSpec((1, rows, D), lambda b, s: (b, s, 0))
    prev = pl.BlockSpec((1, 8, D), lambda b, s: (b, jnp.maximum(s * (rows // 8) - 1, 0), 0))
    row = lambda t: t.reshape(1, -1)
    args = [h, h]
    specs = [tile, prev]
    if has_vres:
        args.append(v_first)
        specs.append(tile)
    weights = [row(ln_g), mu, _bf(w_rkvg), row(w0), _bf(w1), _bf(w2), row(a0), _bf(a1), _bf(a2)]
    if has_vres:
        weights += [row(vres[0]), _bf(vres[1]), _bf(vres[2])]
    args += weights
    specs += [_const_spec(w.shape) for w in weights]
    out = [jax.ShapeDtypeStruct((B, S, D), jnp.float32 if i == 1 else ACT_DTYPE) for i in range(6)]
    return pl.pallas_call(
        functools.partial(_rwkv_proj_body, has_vres),
        grid=(B, S // rows),
        in_specs=specs,
        out_specs=[tile] * 6,
        out_shape=out,
        compiler_params=_params(56, "parallel", "arbitrary"),
        name="rwkv_proj_vres" if has_vres else "rwkv_proj",
    )(*args)


def _wkv_body(r_ref, wl_ref, k_ref, v_ref, a_ref, kk_ref, ka_ref, rk_ref, gng_ref, gnb_ref, out_ref, state_ref):
    C = CHUNK
    n_pairs = r_ref.shape[2] // LANES
    n_chunks = r_ref.shape[1] // C

    @pl.when(pl.program_id(1) == 0)
    def _():
        state_ref[...] = jnp.zeros_like(state_ref)

    lane = lax.broadcasted_iota(jnp.int32, (1, LANES), 1)
    head0 = lane < RW_HEAD
    rowc = lax.broadcasted_iota(jnp.int32, (C, LANES), 0)
    colc = lax.broadcasted_iota(jnp.int32, (C, LANES), 1) % RW_HEAD
    strict = rowc > colc
    incl2 = jnp.concatenate([rowc >= colc, rowc >= colc], axis=1)
    eye_pair = (rowc == colc).astype(jnp.float32)
    rr = lax.broadcasted_iota(jnp.int32, (LANES, LANES), 0)
    cc = lax.broadcasted_iota(jnp.int32, (LANES, LANES), 1)
    same_head = (rr // RW_HEAD) == (cc // RW_HEAD)
    ones_bd = jnp.where(same_head, 1.0, 0.0).astype(jnp.bfloat16)
    eye_full = (rr == cc).astype(jnp.float32)
    tr = lax.broadcasted_iota(jnp.int32, (C, C), 0)
    tc = lax.broadcasted_iota(jnp.int32, (C, C), 1)
    tri = jnp.where(tr >= tc, 1.0, 0.0).astype(jnp.bfloat16)

    def bstack(x):
        xb = _bf(x)
        zero = jnp.zeros_like(xb)
        return jnp.concatenate([jnp.where(head0, xb, zero), jnp.where(head0, zero, xb)], axis=0)

    def head_sums(xs):
        return [jnp.where(head0,
                          jnp.sum(jnp.where(head0, x, 0.0), axis=-1, keepdims=True),
                          jnp.sum(jnp.where(head0, 0.0, x), axis=-1, keepdims=True)) for x in xs]

    def pmm(x_pair, y):
        return lax.dot_general(_bf(x_pair), bstack(y), _NN, preferred_element_type=jnp.float32)

    def prefix_sum(wl):
        w1 = _bf(wl)
        rem = wl - w1.astype(jnp.float32)
        w2 = _bf(rem)
        w3 = _bf(rem - w2.astype(jnp.float32))
        return (jnp.dot(tri, w1, preferred_element_type=jnp.float32)
                + jnp.dot(tri, w2, preferred_element_type=jnp.float32)
                + jnp.dot(tri, w3, preferred_element_type=jnp.float32))

    pairs = range(n_pairs)
    lanes = [slice(p * LANES, (p + 1) * LANES) for p in pairs]
    state = [state_ref[p] for p in pairs]

    def chunk_steps(ci):
        t0 = ci * C
        wl_all = wl_ref[0, t0:t0 + C, :]
        cw_all = prefix_sum(wl_all)
        r = [r_ref[0, t0:t0 + C, ls].astype(jnp.float32) for ls in lanes]
        k = [k_ref[0, t0:t0 + C, ls].astype(jnp.float32) for ls in lanes]
        v = [v_ref[0, t0:t0 + C, ls].astype(jnp.float32) for ls in lanes]
        a = [a_ref[0, t0:t0 + C, ls].astype(jnp.float32) for ls in lanes]
        kk = [k[p] * kk_ref[:, lanes[p]] for p in pairs]
        ss = head_sums([kk[p] * kk[p] for p in pairs])
        yield
        k = [k[p] * (1.0 + (a[p] - 1.0) * ka_ref[:, lanes[p]]) for p in pairs]
        rk = head_sums([r[p] * k[p] * rk_ref[:, lanes[p]] for p in pairs])
        yield
        kk = [kk[p] * lax.rsqrt(jnp.maximum(ss[p], 1e-24)) for p in pairs]
        cw = [cw_all[:, ls] for ls in lanes]
        wc = [cw[p][C - 1:C, :] for p in pairs]
        r_t = [r[p] * jnp.exp(cw[p]) for p in pairs]
        a_t = [-kk[p] * jnp.exp(cw[p] - wl_all[:, lanes[p]]) for p in pairs]
        e_end = [jnp.exp(wc[p] - cw[p]) for p in pairs]
        b_e = [kk[p] * a[p] * e_end[p] for p in pairs]
        k_e = [k[p] * e_end[p] for p in pairs]
        e_up = [jnp.exp(-wc[p]) for p in pairs]
        am = [lax.dot_general(_bf(jnp.concatenate([a_t[p] * e_up[p], r_t[p] * e_up[p]], axis=0)),
                              jnp.concatenate([bstack(b_e[p]), bstack(k_e[p])], axis=0),
                              _NT, preferred_element_type=jnp.float32) for p in pairs]
        yield
        lmat =[jnp.where(strict, am[p][:C, :LANES], 0.0) for p in pairs]
        a_ak = [jnp.where(strict, am[p][:C, LANES:], 0.0) for p in pairs]
        a_r = [_bf(jnp.where(incl2, am[p][C:, :], 0.0)) for p in pairs]
        x = [pmm(a_ak[p], v[p]) for p in pairs]
        yield
        tmat = [eye_pair + lmat[p] for p in pairs]
        pw = [pmm(lmat[p], lmat[p]) for p in pairs]
        yield
        for _ in range(4):
            both = [pmm(jnp.concatenate([tmat[p], pw[p]], axis=0), pw[p]) for p in pairs]
            tmat = [tmat[p] + both[p][:C, :] for p in pairs]
            pw = [both[p][C:, :] for p in pairs]
            yield
        tmat = [tmat[p] + pmm(tmat[p], pw[p]) for p in pairs]
        yield
        apul = [lax.dot_general(_bf(tmat[p]), jnp.concatenate([bstack(a_t[p]), bstack(x[p])], axis=1),
                                _NN, preferred_element_type=jnp.float32) for p in pairs]
        yield
        vb =[bstack(v[p]) for p in pairs]
        zb = jnp.zeros((LANES, LANES), jnp.bfloat16)
        rpyl = [lax.dot_general(a_r[p],
                                jnp.concatenate([jnp.concatenate([bstack(apul[p][:, :LANES]), zb], axis=0),
                                                 jnp.concatenate([bstack(apul[p][:, LANES:]), vb[p]], axis=0)], axis=1),
                                _NN, preferred_element_type=jnp.float32) for p in pairs]
        zc = jnp.zeros((C, LANES), jnp.bfloat16)
        mg = [lax.dot_general(_bf(jnp.concatenate([b_e[p], k_e[p]], axis=0)),
                              jnp.concatenate([_bf(apul[p]), jnp.concatenate([zc, _bf(v[p])], axis=1)], axis=0),
                              _TN, preferred_element_type=jnp.float32) for p in pairs]
        yield
        r_p = [r_t[p] + rpyl[p][:, :LANES] for p in pairs]
        m_d = [eye_full * jnp.exp(wc[p]) + jnp.where(same_head, mg[p][:, :LANES], 0.0) for p in pairs]
        ys = [lax.dot_general(_bf(jnp.concatenate([r_p[p], m_d[p]], axis=0)), _bf(state[p]), _NN,
                              preferred_element_type=jnp.float32) for p in pairs]
        for p in pairs:
            state[p] = ys[p][C:, :] + jnp.where(same_head, mg[p][:, LANES:], 0.0)
        y = [ys[p][:C, :] + rpyl[p][:, LANES:] for p in pairs]
        inv_n = 1.0 / RW_HEAD
        mean = head_sums(y)
        yield
        yc = [y[p] - mean[p] * inv_n for p in pairs]
        var = head_sums([yc[p] * yc[p] for p in pairs])
        yield
        for p in pairs:
            yn = yc[p] * lax.rsqrt(var[p] * inv_n + GN_EPS) * gng_ref[:, lanes[p]] + gnb_ref[:, lanes[p]]
            out_ref[0, t0:t0 + C, lanes[p]] = yn + rk[p] * v[p]

    steps = [chunk_steps(ci) for ci in range(n_chunks)]
    live = set(range(n_chunks))
    tick = 0
    while live:
        for ci in sorted(live):
            if tick >= ci * WKV_STAGGER:
                try:
                    next(steps[ci])
                except StopIteration:
                    live.discard(ci)
        tick += 1
    for p in pairs:
        state_ref[p] = state[p]


def _wkv(r, wl, k, v, a, k_k, k_a, r_k, gn_g, gn_b):
    B, S, D = r.shape
    rows = WKV_ROWS
    tile = pl.BlockSpec((1, rows, D), lambda b, s: (b, s, 0))
    vec = _const_spec((1, D))
    row = lambda t: t.reshape(1, -1)
    return pl.pallas_call(
        _wkv_body,
        grid=(B, S // rows),
        in_specs=[tile] * 5 + [vec] * 5,
        out_specs=tile,
        out_shape=jax.ShapeDtypeStruct((B, S, D), jnp.float32),
        scratch_shapes=[pltpu.VMEM((D // LANES, LANES, LANES), jnp.float32)],
        compiler_params=_params(48, "parallel", "arbitrary"),
        name="wkv7_chunked",
    )(r, wl, k, v, a, row(k_k), row(k_a), row(r_k), row(gn_g), row(gn_b))


def _out_body(mode, *refs):
    if mode == "kv":
        (z_ref, gate_ref, h_ref, p_ref, wo_ref, pw_ref, pg_ref, pwg_ref, kvg_ref, wkv_ref,
         h_out, k1_out, v1_out, k4_out, v4_out, k16_out, v16_out, kv_scr, kv_scr4) = refs
    elif mode == "final":
        (z_ref, gate_ref, h_ref, p_ref, wo_ref, pw_ref, pg_ref, pwg_ref, fg_ref, h_out) = refs
    else:
        (z_ref, gate_ref, h_ref, p_ref, wo_ref, pw_ref, pg_ref, pwg_ref, h_out) = refs
    gate = gate_ref[0].astype(jnp.float32)
    zg = z_ref[0] * (gate * _sigmoid(gate))
    h = h_ref[0] + _dot(zg, wo_ref[...])
    pgate = _sigmoid(_dot(_rms(h, pg_ref[...]), pwg_ref[...]))
    h = h + _dot(p_ref[0], pw_ref[...]) * pgate
    if mode == "final":
        h_out[0] = _rms(h, fg_ref[...])
    else:
        h_out[0] = h
    if mode == "kv":
        xn = _bf(_rms(h, kvg_ref[...]))
        rows = xn.shape[0]
        heads_tiles = ATT_WIDTH // LANES
        for part, (dst1, dst4, dst16) in enumerate(((k1_out, k4_out, k16_out), (v1_out, v4_out, v16_out))):
            kv = jnp.dot(xn, wkv_ref[:, part * ATT_WIDTH:(part + 1) * ATT_WIDTH], preferred_element_type=jnp.float32)
            dst1[0] = _bf(kv)
            for t in range(heads_tiles):
                ls = slice(t * LANES, (t + 1) * LANES)
                slab = part * heads_tiles + t
                kv_scr[slab] = kv[:, ls]
                for lo in range(4):
                    by4 = kv_scr[slab, pl.ds(lo, rows // 4, stride=4), :]
                    dst4[0, lo, :, ls] = _bf(by4)
                    kv_scr4[slab, lo] = by4
                for lo in range(4):
                    for hi in range(4):
                        dst16[0, lo + 4 * hi, :, ls] = _bf(kv_scr4[slab, lo, pl.ds(hi, rows // 16, stride=4), :])


def _out_layer(z, gate, h, p, layer, w_o, ple_w, ple_g, ple_wg, mode="plain", extra=()):
    B, S, D = h.shape
    rows = OUT_ROWS
    tile = pl.BlockSpec((1, rows, D), lambda b, s: (b, s, 0))
    ptile = pl.BlockSpec((None, 1, rows, PLE_DIM), lambda b, s: (layer, b, s, 0))
    weights = [_bf(w_o), _bf(ple_w), ple_g.reshape(1, -1), _bf(ple_wg)]
    if mode == "kv":
        weights += [extra[0].reshape(1, -1), _bf(extra[1])]
    elif mode == "final":
        weights += [extra[0].reshape(1, -1)]
    out_specs = [tile]
    out_shape = [jax.ShapeDtypeStruct((B, S, D), jnp.float32)]
    scratch = []
    if mode == "kv":
        out_specs += [tile, tile]
        out_shape += [jax.ShapeDtypeStruct((B, S, ATT_WIDTH), jnp.bfloat16)] * 2
        for d in DILATIONS[1:]:
            out_specs += [pl.BlockSpec((1, d, rows // d, ATT_WIDTH), lambda b, s: (b, 0, s, 0))] * 2
            out_shape += [jax.ShapeDtypeStruct((B, d, S // d, ATT_WIDTH), jnp.bfloat16)] * 2
        slabs = 2 * ATT_WIDTH // LANES
        scratch = [pltpu.VMEM((slabs, rows, LANES), jnp.float32), pltpu.VMEM((slabs, 4, rows // 4, LANES), jnp.float32)]
    res = pl.pallas_call(
        functools.partial(_out_body, mode),
        grid=(B, S // rows),
        in_specs=[tile, tile, tile, ptile] + [_const_spec(w.shape) for w in weights],
        out_specs=out_specs,
        out_shape=out_shape,
        scratch_shapes=scratch,
        compiler_params=_params(56, "parallel", "arbitrary"),
        name="out_layer_" + mode,
    )(z, gate, h, p, *weights)
    return res if mode == "kv" else res[0]


def _qproj_body(h_ref, g_ref, w_ref, q0_out, q1_out, q2_out, gate_out, q_scr, q_scr4):
    xn = _bf(_rms(h_ref[0], g_ref[...]))
    L = BLOCK
    heads_tiles = ATT_WIDTH // LANES
    for g in range(N_GROUPS):
        proj = jnp.dot(xn, w_ref[:, g * ATT_WIDTH:(g + 1) * ATT_WIDTH], preferred_element_type=jnp.float32)
        proj = proj * (ATT_HEAD ** -0.5 * LOG2E)
        for t in range(heads_tiles):
            ls = slice(t * LANES, (t + 1) * LANES)
            slab = g * heads_tiles + t
            q_scr[slab] = proj[:, ls]
            for lo in range(4):
                q_scr4[slab, lo] = q_scr[slab, pl.ds(lo, QUARTER // 4, stride=4), :]
            if g == 0:
                for c in range(QUARTER // L):
                    for hi in range(4):
                        for lo in range(0, 4, 2):
                            two = jnp.concatenate(
                                [q_scr4[slab, lo, pl.ds(c * (L // 4) + hi, 8, stride=4), :],
                                 q_scr4[slab, lo + 1, pl.ds(c * (L // 4) + hi, 8, stride=4), :]], axis=0)
                            res = lo + 4 * hi
                            q0_out[0, 0, c, res * 8:(res + 2) * 8, ls] = _bf(two)
            elif g == 1:
                for lo in range(4):
                    for hi in range(4):
                        q1_out[0, 0, 0, lo, hi * 32:(hi + 1) * 32, ls] = _bf(
                            q_scr4[slab, lo, pl.ds(hi, QUARTER // 16, stride=4), :])
            else:
                for lo in range(4):
                    for hi in range(4):
                        q2_out[0, 0, lo + 4 * hi, :, ls] = _bf(q_scr4[slab, lo, pl.ds(hi, QUARTER // 16, stride=4), :])
    gate_out[0] = jnp.dot(xn, w_ref[:, Q_WIDTH:], preferred_element_type=jnp.float32).astype(gate_out.dtype)


def _qproj(h, ln_g, w_in):
    B, S, D = h.shape
    rows = QUARTER
    n_tiles = S // ATT_ROWS
    tile = pl.BlockSpec((1, rows, D), lambda b, s: (b, s, 0))
    q_shape = jax.ShapeDtypeStruct((B, n_tiles, 16, BLOCK, ATT_WIDTH), jnp.bfloat16)
    q0, q1, q2, gate = pl.pallas_call(
        _qproj_body,
        grid=(B, S // rows),
        in_specs=[tile, _const_spec((1, D)), _const_spec(w_in.shape)],
        out_specs=[pl.BlockSpec((1, 1, 4, BLOCK, ATT_WIDTH), lambda b, s: (b, s // 4, s % 4, 0, 0)),
                   pl.BlockSpec((1, 1, 1, 4, BLOCK, ATT_WIDTH), lambda b, s: (b, s // 4, s % 4, 0, 0, 0)),
                   pl.BlockSpec((1, 1, 16, BLOCK // 4, ATT_WIDTH), lambda b, s: (b, s // 4, 0, s % 4, 0)),
                   tile],
        out_shape=[q_shape,
                   jax.ShapeDtypeStruct((B, n_tiles, 4, 4, BLOCK, ATT_WIDTH), jnp.bfloat16),
                   q_shape,
                   jax.ShapeDtypeStruct((B, S, ATT_WIDTH), ACT_DTYPE)],
        scratch_shapes=[pltpu.VMEM((Q_WIDTH // LANES, rows, LANES), jnp.float32),
                        pltpu.VMEM((Q_WIDTH // LANES, 4, rows // 4, LANES), jnp.float32)],
        compiler_params=_params(56, "parallel", "arbitrary"),
        name="att_qproj",
    )(h, ln_g.reshape(1, -1), _bf(w_in))
    return (q0, q1.reshape(q0.shape), q2), gate


def _att_body(q0_ref, q1_ref, q2_ref, k1_ref, v1_ref, k4_ref, v4_ref, k16_ref, v16_ref, out_ref,
              acc_ref, l_ref, m_ref):
    L = BLOCK
    head = pl.program_id(1)
    tile = pl.program_id(2)
    n_units = q0_ref.shape[0]
    slope = jnp.exp2(-(head + 1).astype(jnp.float32)) * LOG2E
    rho = lax.broadcasted_iota(jnp.int32, (L, 2 * L), 0)
    kj = lax.broadcasted_iota(jnp.int32, (L, 2 * L), 1)
    ones = jnp.ones((2 * L, ATT_HEAD), jnp.bfloat16)

    def bias_pair(pos):
        delta = L + pos - kj
        regular = jnp.where((delta >= 0) & (delta <= L), -slope * delta.astype(jnp.float32), NEG_BIG)
        delta = pos - kj
        first = jnp.where(delta >= 0, -slope * delta.astype(jnp.float32), NEG_BIG)
        return regular, first

    def batch_steps(q_ref, us, unit_io, regular, first):
        io = [unit_io(u) for u in us]
        s = [lax.dot_general(q_ref[u], k, _NT, preferred_element_type=jnp.float32)
             + (regular if is_first is None else jnp.where(is_first, first, regular))
             for u, (k, _, is_first, _) in zip(us, io)]
        yield
        m = [jnp.max(s_, axis=-1, keepdims=True) for s_ in s]
        p = [_bf(jnp.exp2(s_ - m_)) for s_, m_ in zip(s, m)]
        yield
        pv = [lax.dot_general(p_, jnp.concatenate([v, ones], axis=1), _NN, preferred_element_type=jnp.float32)
              for p_, (_, v, _, _) in zip(p, io)]
        yield
        for pv_, m_, (_, _, _, store) in zip(pv, m, io):
            store(acc_ref, pv_[:, :ATT_HEAD])
            store(l_ref, pv_[:, ATT_HEAD:])
            store(m_ref, jnp.broadcast_to(m_, (L, ATT_HEAD)))

    def start_of(block):
        return pl.multiple_of(jnp.maximum(block - 1, 0) * L, L)

    def io_1(u):
        block = tile * n_units + u
        rows = pl.ds(start_of(block), 2 * L)

        def store(ref, val):
            for res in range(16):
                ref[0, res, u * 8:(u + 1) * 8, :] = val[res * 8:(res + 1) * 8, :]
        return k1_ref[rows, :], v1_ref[rows, :], (block == 0 if u == 0 else None), store

    def io_4(u):
        c, res4 = u // 4, u % 4
        block = tile * 4 + c
        rows = pl.ds(start_of(block), 2 * L)

        def store(ref, val):
            for j in range(4):
                ref[1, 4 * j + res4, c * 32:(c + 1) * 32, :] = val[j * 32:(j + 1) * 32, :]
        return k4_ref[res4, rows, :], v4_ref[res4, rows, :], (block == 0 if c == 0 else None), store

    def io_16(u):
        rows = pl.ds(start_of(tile), 2 * L)

        def store(ref, val):
            ref[2, u] = val
        return k16_ref[u, rows, :], v16_ref[u, rows, :], tile == 0, store

    groups = ((q0_ref, (rho % 8) * 16 + rho // 8, io_1),
              (q1_ref, (rho % 32) * 4 + rho // 32, io_4),
              (q2_ref, rho, io_16))
    batches = []
    for q_ref, pos, unit_io in groups:
        regular, first = bias_pair(pos)
        for i in range(n_units // ATT_UNROLL):
            us = range(i * ATT_UNROLL, (i + 1) * ATT_UNROLL)
            batches.append(batch_steps(q_ref, us, unit_io, regular, first))
    live = list(range(len(batches)))
    tick = 0
    while live:
        for i in list(live):
            if tick >= i:
                try:
                    next(batches[i])
                except StopIteration:
                    live.remove(i)
        tick += 1

    m_all = jnp.maximum(jnp.maximum(m_ref[0], m_ref[1]), m_ref[2])
    num = jnp.zeros(m_all.shape, jnp.float32)
    den = jnp.zeros(m_all.shape, jnp.float32)
    for g in range(N_GROUPS):
        e = jnp.exp2(m_ref[g] - m_all)
        num = num + e * acc_ref[g]
        den = den + e * l_ref[g]
    merged = num / den
    for res in range(16):
        out_ref[0, pl.ds(res, L, stride=16), :] = merged[res]


def _attention(q, kv):
    B, S, _ = kv[0].shape
    rows = ATT_ROWS
    q_spec = pl.BlockSpec((None, None, 16, BLOCK, ATT_HEAD), lambda b, h, t: (b, t, 0, 0, h))
    kv_specs = [pl.BlockSpec((None, S, ATT_HEAD), lambda b, h, t: (b, 0, h))] * 2
    for d in DILATIONS[1:]:
        kv_specs += [pl.BlockSpec((None, d, S // d, ATT_HEAD), lambda b, h, t: (b, 0, 0, h))] * 2
    scratch = pltpu.VMEM((N_GROUPS, 16, BLOCK, ATT_HEAD), jnp.float32)
    return pl.pallas_call(
        _att_body,
        grid=(B, KV_HEADS, S // rows),
        in_specs=[q_spec] * N_GROUPS + kv_specs,
        out_specs=pl.BlockSpec((1, rows, ATT_HEAD), lambda b, h, t: (b, t, h)),
        out_shape=jax.ShapeDtypeStruct((B, S, ATT_WIDTH), jnp.float32),
        scratch_shapes=[scratch, scratch, scratch],
        compiler_params=_params(48, "parallel", "parallel", "arbitrary"),
        name="dilated_attention",
    )(*q, *kv)


def kernel(x, p, a_ln_g, a_mu, a_w_rkvg, a_w0, a_w1, a_w2, a_a0, a_a1, a_a2, a_v0, a_v1, a_v2, a_k_k, a_k_a, a_r_k, a_gn_g, a_gn_b, a_w_o, kv_ln_g, w_kv, b_ln_g, b_w_in, b_w_o, ple_w, ple_gate_ln_g, ple_w_gate, final_ln_g):
    n_a = a_ln_g.shape[0]
    depth = p.shape[0]
    assert x.shape[1] % ATT_ROWS == 0 and x.shape[2] == D_MODEL
    h = x
    v_first = None
    kv_att = None
    for i in range(depth):
        if i < n_a:
            vres = None if i == 0 else (a_v0[i - 1], a_v1[i - 1], a_v2[i - 1])
            r, wl, k, v, a, gate = _rwkv_proj(h, v_first, a_ln_g[i], a_mu[i], a_w_rkvg[i], a_w0[i], a_w1[i], a_w2[i],
                                              a_a0[i], a_a1[i], a_a2[i], vres)
            if i == 0:
                v_first = v
            z = _wkv(r, wl, k, v, a, a_k_k[i], a_k_a[i], a_r_k[i], a_gn_g[i], a_gn_b[i])
            w_o = a_w_o[i]
        else:
            j = i - n_a
            q, gate = _qproj(h, b_ln_g[j], b_w_in[j])
            z = _attention(q, kv_att)
            w_o = b_w_o[j]
        if i == n_a - 1:
            h, *kv_att = _out_layer(z, gate, h, p, i, w_o, ple_w[i], ple_gate_ln_g[i], ple_w_gate[i],
                                    "kv", (kv_ln_g, w_kv))
        elif i == depth - 1:
            h = _out_layer(z, gate, h, p, i, w_o, ple_w[i], ple_gate_ln_g[i], ple_w_gate[i], "final", (final_ln_g,))
        else:
            h = _out_layer(z, gate, h, p, i, w_o, ple_w[i], ple_gate_ln_g[i], ple_w_gate[i])
    return h
```

```python
import functools

import jax
import jax.numpy as jnp
from jax import lax
from jax.experimental import pallas as pl
from jax.experimental.pallas import tpu as pltpu

D_MODEL = 1024
PLE_DIM = 256
NORM_EPS = 1e-6
GN_EPS = 64e-5
RW_HEAD = 64
ATT_HEAD = 128
KV_HEADS = 8
ATT_WIDTH = KV_HEADS * ATT_HEAD
DILATIONS = (1, 4, 16)
N_GROUPS = len(DILATIONS)
Q_WIDTH = N_GROUPS * ATT_WIDTH
BLOCK = 128
LANES = 128
CHUNK = 64
NEG_BIG = -1e30
LOG2E = 1.4426950408889634
ACT_DTYPE = jnp.bfloat16

PROJ_ROWS = 512
OUT_ROWS = 512
WKV_ROWS = 512
WKV_STAGGER = 3
ATT_ROWS = 2048
QUARTER = 4 * BLOCK
ATT_UNROLL = 4

_NN = (((1,), (0,)), ((), ()))
_NT = (((1,), (1,)), ((), ()))
_TN = (((0,), (0,)), ((), ()))


def _bf(x):
    return x.astype(jnp.bfloat16)


def _dot(a, b, dims=_NN):
    return lax.dot_general(_bf(a), _bf(b), dims, preferred_element_type=jnp.float32)


def _dot_hilo(a, b_bf16):
    hi = _bf(a)
    lo = _bf(a - hi.astype(jnp.float32))
    return (lax.dot_general(hi, b_bf16, _NN, preferred_element_type=jnp.float32)
            + lax.dot_general(lo, b_bf16, _NN, preferred_element_type=jnp.float32))


def _rms(x, g):
    return x * lax.rsqrt(jnp.mean(x * x, axis=-1, keepdims=True) + NORM_EPS) * g


def _sigmoid(x):
    return 1.0 / (1.0 + jnp.exp(-x))


def _params(vmem_mb, *sem):
    return pltpu.CompilerParams(dimension_semantics=sem, vmem_limit_bytes=vmem_mb * 1024 * 1024)


def _const_spec(shape):
    zeros = (0,) * len(shape)
    return pl.BlockSpec(shape, lambda *_: zeros)


def _rwkv_proj_body(has_vres, *refs):
    if has_vres:
        (h_ref, hprev_ref, vfirst_ref, g_ref, mu_ref, w_ref, w0_ref, w1_ref, w2_ref, a0_ref, a1_ref, a2_ref,
         v0_ref, v1_ref, v2_ref, r_out, wl_out, k_out, v_out, a_out, gate_out) = refs
    else:
        (h_ref, hprev_ref, g_ref, mu_ref, w_ref, w0_ref, w1_ref, w2_ref, a0_ref, a1_ref, a2_ref,
         r_out, wl_out, k_out, v_out, a_out, gate_out) = refs
    g = g_ref[...]
    x = _rms(h_ref[0], g)
    rows = x.shape[0]
    xlast = _rms(hprev_ref[0][7:8, :], g)
    xlast = jnp.where(pl.program_id(1) > 0, xlast, 0.0)
    row = lax.broadcasted_iota(jnp.int32, (rows, 1), 0)
    xprev = jnp.where(row == 0, xlast, pltpu.roll(x, 1, axis=0))
    xx = xprev - x

    def mix(n):
        return _bf(x + xx * mu_ref[n:n + 1, :])

    r_out[0] = jnp.dot(mix(0), w_ref[0], preferred_element_type=jnp.float32).astype(r_out.dtype)
    k_out[0] = jnp.dot(mix(1), w_ref[1], preferred_element_type=jnp.float32).astype(k_out.dtype)
    xv = mix(2)
    v = jnp.dot(xv, w_ref[2], preferred_element_type=jnp.float32)
    if has_vres:
        vmix = _sigmoid(v0_ref[...] + _dot(jnp.dot(xv, v1_ref[...], preferred_element_type=jnp.float32),
                                           v2_ref[...]))
        v = v + (vfirst_ref[0].astype(jnp.float32) - v) * vmix
    v_out[0] = v.astype(v_out.dtype)
    gate_out[0] = jnp.dot(mix(3), w_ref[3], preferred_element_type=jnp.float32).astype(gate_out.dtype)
    lw = jnp.tanh(jnp.dot(mix(4), w1_ref[...], preferred_element_type=jnp.float32))
    z = -(w0_ref[...] + _dot(lw, w2_ref[...]))
    softplus = jnp.maximum(z, 0.0) + jnp.log(1.0 + jnp.exp(-jnp.abs(z)))
    wl_out[0] = -jnp.exp(-softplus - 0.5)
    la = jnp.dot(mix(5), a1_ref[...], preferred_element_type=jnp.float32)
    a_out[0] = _sigmoid(a0_ref[...] + _dot(la, a2_ref[...])).astype(a_out.dtype)


def _rwkv_proj(h, v_first, ln_g, mu, w_rkvg, w0, w1, w2, a0, a1, a2, vres):
    B, S, D = h.shape
    rows = PROJ_ROWS
    has_vres = vres is not None
    tile = pl.BlockSpec((1, rows, D), lambda b, s: (b, s, 0))
    prev = pl.BlockSpec((1, 8, D), lambda b, s: (b, jnp.maximum(s * (rows // 8) - 1, 0), 0))
    row = lambda t: t.reshape(1, -1)
    args = [h, h]
    specs = [tile, prev]
    if has_vres:
        args.append(v_first)
        specs.append(tile)
    weights = [row(ln_g), mu, _bf(w_rkvg), row(w0), _bf(w1), _bf(w2), row(a0), _bf(a1), _bf(a2)]
    if has_vres:
        weights += [row(vres[0]), _bf(vres[1]), _bf(vres[2])]
    args += weights
    specs += [_const_spec(w.shape) for w in weights]
    out = [jax.ShapeDtypeStruct((B, S, D), jnp.float32 if i == 1 else ACT_DTYPE) for i in range(6)]
    return pl.pallas_call(
        functools.partial(_rwkv_proj_body, has_vres),
        grid=(B, S // rows),
        in_specs=specs,
        out_specs=[tile] * 6,
        out_shape=out,
        compiler_params=_params(56, "parallel", "arbitrary"),
        name="rwkv_proj_vres" if has_vres else "rwkv_proj",
    )(*args)


def _wkv_body(r_ref, wl_ref, k_ref, v_ref, a_ref, kk_ref, ka_ref, rk_ref, gng_ref, gnb_ref, out_ref, state_ref):
    C = CHUNK
    n_pairs = r_ref.shape[2] // LANES
    n_chunks = r_ref.shape[1] // C

    @pl.when(pl.program_id(1) == 0)
    def _():
        state_ref[...] = jnp.zeros_like(state_ref)

    lane = lax.broadcasted_iota(jnp.int32, (1, LANES), 1)
    head0 = lane < RW_HEAD
    rowc = lax.broadcasted_iota(jnp.int32, (C, LANES), 0)
    colc = lax.broadcasted_iota(jnp.int32, (C, LANES), 1) % RW_HEAD
    strict = rowc > colc
    incl2 = jnp.concatenate([rowc >= colc, rowc >= colc], axis=1)
    eye_pair = (rowc == colc).astype(jnp.float32)
    rr = lax.broadcasted_iota(jnp.int32, (LANES, LANES), 0)
    cc = lax.broadcasted_iota(jnp.int32, (LANES, LANES), 1)
    same_head = (rr // RW_HEAD) == (cc // RW_HEAD)
    ones_bd = jnp.where(same_head, 1.0, 0.0).astype(jnp.bfloat16)
    eye_full = (rr == cc).astype(jnp.float32)
    tr = lax.broadcasted_iota(jnp.int32, (C, C), 0)
    tc = lax.broadcasted_iota(jnp.int32, (C, C), 1)
    tri = jnp.where(tr >= tc, 1.0, 0.0).astype(jnp.bfloat16)

    def bstack(x):
        xb = _bf(x)
        zero = jnp.zeros_like(xb)
        return jnp.concatenate([jnp.where(head0, xb, zero), jnp.where(head0, zero, xb)], axis=0)

    def head_sums(xs):
        return [jnp.where(head0,
                          jnp.sum(jnp.where(head0, x, 0.0), axis=-1, keepdims=True),
                          jnp.sum(jnp.where(head0, 0.0, x), axis=-1, keepdims=True)) for x in xs]

    def pmm(x_pair, y):
        return lax.dot_general(_bf(x_pair), bstack(y), _NN, preferred_element_type=jnp.float32)

    def prefix_sum(wl):
        w1 = _bf(wl)
        rem = wl - w1.astype(jnp.float32)
        w2 = _bf(rem)
        w3 = _bf(rem - w2.astype(jnp.float32))
        return (jnp.dot(tri, w1, preferred_element_type=jnp.float32)
                + jnp.dot(tri, w2, preferred_element_type=jnp.float32)
                + jnp.dot(tri, w3, preferred_element_type=jnp.float32))

    pairs = range(n_pairs)
    lanes = [slice(p * LANES, (p + 1) * LANES) for p in pairs]
    state = [state_ref[p] for p in pairs]

    def chunk_steps(ci):
        t0 = ci * C
        wl_all = wl_ref[0, t0:t0 + C, :]
        cw_all = prefix_sum(wl_all)
        r = [r_ref[0, t0:t0 + C, ls].astype(jnp.float32) for ls in lanes]
        k = [k_ref[0, t0:t0 + C, ls].astype(jnp.float32) for ls in lanes]
        v = [v_ref[0, t0:t0 + C, ls].astype(jnp.float32) for ls in lanes]
        a = [a_ref[0, t0:t0 + C, ls].astype(jnp.float32) for ls in lanes]
        kk = [k[p] * kk_ref[:, lanes[p]] for p in pairs]
        ss = head_sums([kk[p] * kk[p] for p in pairs])
        yield
        k = [k[p] * (1.0 + (a[p] - 1.0) * ka_ref[:, lanes[p]]) for p in pairs]
        rk = head_sums([r[p] * k[p] * rk_ref[:, lanes[p]] for p in pairs])
        yield
        kk = [kk[p] * lax.rsqrt(jnp.maximum(ss[p], 1e-24)) for p in pairs]
        cw = [cw_all[:, ls] for ls in lanes]
        wc = [cw[p][C - 1:C, :] for p in pairs]
        r_t = [r[p] * jnp.exp(cw[p]) for p in pairs]
        a_t = [-kk[p] * jnp.exp(cw[p] - wl_all[:, lanes[p]]) for p in pairs]
        e_end = [jnp.exp(wc[p] - cw[p]) for p in pairs]
        b_e = [kk[p] * a[p] * e_end[p] for p in pairs]
        k_e = [k[p] * e_end[p] for p in pairs]
        e_up = [jnp.exp(-wc[p]) for p in pairs]
        am = [lax.dot_general(_bf(jnp.concatenate([a_t[p] * e_up[p], r_t[p] * e_up[p]], axis=0)),
                              jnp.concatenate([bstack(b_e[p]), bstack(k_e[p])], axis=0),
                              _NT, preferred_element_type=jnp.float32) for p in pairs]
        yield
        lmat =[jnp.where(strict, am[p][:C, :LANES], 0.0) for p in pairs]
        a_ak = [jnp.where(strict, am[p][:C, LANES:], 0.0) for p in pairs]
        a_r = [_bf(jnp.where(incl2, am[p][C:, :], 0.0)) for p in pairs]
        x = [pmm(a_ak[p], v[p]) for p in pairs]
        yield
        tmat = [eye_pair + lmat[p] for p in pairs]
        pw = [pmm(lmat[p], lmat[p]) for p in pairs]
        yield
        for _ in range(4):
            both = [pmm(jnp.concatenate([tmat[p], pw[p]], axis=0), pw[p]) for p in pairs]
            tmat = [tmat[p] + both[p][:C, :] for p in pairs]
            pw = [both[p][C:, :] for p in pairs]
            yield
        tmat = [tmat[p] + pmm(tmat[p], pw[p]) for p in pairs]
        yield
        apul = [lax.dot_general(_bf(tmat[p]), jnp.concatenate([bstack(a_t[p]), bstack(x[p])], axis=1),
                                _NN, preferred_element_type=jnp.float32) for p in pairs]
        yield
        vb =[bstack(v[p]) for p in pairs]
        zb = jnp.zeros((LANES, LANES), jnp.bfloat16)
        rpyl = [lax.dot_general(a_r[p],
                                jnp.concatenate([jnp.concatenate([bstack(apul[p][:, :LANES]), zb], axis=0),
                                                 jnp.concatenate([bstack(apul[p][:, LANES:]), vb[p]], axis=0)], axis=1),
                                _NN, preferred_element_type=jnp.float32) for p in pairs]
        zc = jnp.zeros((C, LANES), jnp.bfloat16)
        mg = [lax.dot_general(_bf(jnp.concatenate([b_e[p], k_e[p]], axis=0)),
                              jnp.concatenate([_bf(apul[p]), jnp.concatenate([zc, _bf(v[p])], axis=1)], axis=0),
                              _TN, preferred_element_type=jnp.float32) for p in pairs]
        yield
        r_p = [r_t[p] + rpyl[p][:, :LANES] for p in pairs]
        m_d = [eye_full * jnp.exp(wc[p]) + jnp.where(same_head, mg[p][:, :LANES], 0.0) for p in pairs]
        ys = [lax.dot_general(_bf(jnp.concatenate([r_p[p], m_d[p]], axis=0)), _bf(state[p]), _NN,
                              preferred_element_type=jnp.float32) for p in pairs]
        for p in pairs:
            state[p] = ys[p][C:, :] + jnp.where(same_head, mg[p][:, LANES:], 0.0)
        y = [ys[p][:C, :] + rpyl[p][:, LANES:] for p in pairs]
        inv_n = 1.0 / RW_HEAD
        mean = head_sums(y)
        yield
        yc = [y[p] - mean[p] * inv_n for p in pairs]
        var = head_sums([yc[p] * yc[p] for p in pairs])
        yield
        for p in pairs:
            yn = yc[p] * lax.rsqrt(var[p] * inv_n + GN_EPS) * gng_ref[:, lanes[p]] + gnb_ref[:, lanes[p]]
            out_ref[0, t0:t0 + C, lanes[p]] = yn + rk[p] * v[p]

    steps = [chunk_steps(ci) for ci in range(n_chunks)]
    live = set(range(n_chunks))
    tick = 0
    while live:
        for ci in sorted(live):
            if tick >= ci * WKV_STAGGER:
                try:
                    next(steps[ci])
                except StopIteration:
                    live.discard(ci)
        tick += 1
    for p in pairs:
        state_ref[p] = state[p]


def _wkv(r, wl, k, v, a, k_k, k_a, r_k, gn_g, gn_b):
    B, S, D = r.shape
    rows = WKV_ROWS
    tile = pl.BlockSpec((1, rows, D), lambda b, s: (b, s, 0))
    vec = _const_spec((1, D))
    row = lambda t: t.reshape(1, -1)
    return pl.pallas_call(
        _wkv_body,
        grid=(B, S // rows),
        in_specs=[tile] * 5 + [vec] * 5,
        out_specs=tile,
        out_shape=jax.ShapeDtypeStruct((B, S, D), jnp.float32),
        scratch_shapes=[pltpu.VMEM((D // LANES, LANES, LANES), jnp.float32)],
        compiler_params=_params(48, "parallel", "arbitrary"),
        name="wkv7_chunked",
    )(r, wl, k, v, a, row(k_k), row(k_a), row(r_k), row(gn_g), row(gn_b))


def _out_body(mode, *refs):
    if mode == "kv":
        (z_ref, gate_ref, h_ref, p_ref, wo_ref, pw_ref, pg_ref, pwg_ref, kvg_ref, wkv_ref,
         h_out, k1_out, v1_out, k4_out, v4_out, k16_out, v16_out, kv_scr, kv_scr4) = refs
    elif mode == "final":
        (z_ref, gate_ref, h_ref, p_ref, wo_ref, pw_ref, pg_ref, pwg_ref, fg_ref, h_out) = refs
    else:
        (z_ref, gate_ref, h_ref, p_ref, wo_ref, pw_ref, pg_ref, pwg_ref, h_out) = refs
    gate = gate_ref[0].astype(jnp.float32)
    zg = z_ref[0] * (gate * _sigmoid(gate))
    h = h_ref[0] + _dot(zg, wo_ref[...])
    pgate = _sigmoid(_dot(_rms(h, pg_ref[...]), pwg_ref[...]))
    h = h + _dot(p_ref[0], pw_ref[...]) * pgate
    if mode == "final":
        h_out[0] = _rms(h, fg_ref[...])
    else:
        h_out[0] = h
    if mode == "kv":
        xn = _bf(_rms(h, kvg_ref[...]))
        rows = xn.shape[0]
        heads_tiles = ATT_WIDTH // LANES
        for part, (dst1, dst4, dst16) in enumerate(((k1_out, k4_out, k16_out), (v1_out, v4_out, v16_out))):
            kv = jnp.dot(xn, wkv_ref[:, part * ATT_WIDTH:(part + 1) * ATT_WIDTH], preferred_element_type=jnp.float32)
            for t in range(heads_tiles):
                ls = slice(t * LANES, (t + 1) * LANES)
                slab = part * heads_tiles + t
                dst1[0, t] = _bf(kv[:, ls])
                kv_scr[slab] = kv[:, ls]
                for lo in range(4):
                    by4 = kv_scr[slab, pl.ds(lo, rows // 4, stride=4), :]
                    dst4[0, t, lo] = _bf(by4)
                    kv_scr4[slab, lo] = by4
                for lo in range(4):
                    for hi in range(4):
                        dst16[0, t, lo + 4 * hi] = _bf(kv_scr4[slab, lo, pl.ds(hi, rows // 16, stride=4), :])


def _out_layer(z, gate, h, p, layer, w_o, ple_w, ple_g, ple_wg, mode="plain", extra=()):
    B, S, D = h.shape
    rows = OUT_ROWS
    tile = pl.BlockSpec((1, rows, D), lambda b, s: (b, s, 0))
    ptile = pl.BlockSpec((None, 1, rows, PLE_DIM), lambda b, s: (layer, b, s, 0))
    weights = [_bf(w_o), _bf(ple_w), ple_g.reshape(1, -1), _bf(ple_wg)]
    if mode == "kv":
        weights += [extra[0].reshape(1, -1), _bf(extra[1])]
    elif mode == "final":
        weights += [extra[0].reshape(1, -1)]
    out_specs = [tile]
    out_shape = [jax.ShapeDtypeStruct((B, S, D), jnp.float32)]
    scratch = []
    if mode == "kv":
        out_specs += [pl.BlockSpec((1, KV_HEADS, rows, ATT_HEAD), lambda b, s: (b, 0, s, 0))] * 2
        out_shape += [jax.ShapeDtypeStruct((B, KV_HEADS, S, ATT_HEAD), jnp.bfloat16)] * 2
        for d in DILATIONS[1:]:
            out_specs += [pl.BlockSpec((1, KV_HEADS, d, rows // d, ATT_HEAD), lambda b, s: (b, 0, 0, s, 0))] * 2
            out_shape += [jax.ShapeDtypeStruct((B, KV_HEADS, d, S // d, ATT_HEAD), jnp.bfloat16)] * 2
        slabs = 2 * ATT_WIDTH // LANES
        scratch = [pltpu.VMEM((slabs, rows, LANES), jnp.float32), pltpu.VMEM((slabs, 4, rows // 4, LANES), jnp.float32)]
    res = pl.pallas_call(
        functools.partial(_out_body, mode),
        grid=(B, S // rows),
        in_specs=[tile, tile, tile, ptile] + [_const_spec(w.shape) for w in weights],
        out_specs=out_specs,
        out_shape=out_shape,
        scratch_shapes=scratch,
        compiler_params=_params(56, "parallel", "arbitrary"),
        name="out_layer_" + mode,
    )(z, gate, h, p, *weights)
    return res if mode == "kv" else res[0]


def _qproj_body(h_ref, g_ref, w_ref, q0_out, q1_out, q2_out, gate_out, x_scr, x_scr4, xp_scr):
    x = _rms(h_ref[0], g_ref[...])
    L = BLOCK
    n = QUARTER // 16
    for t in range(x.shape[1] // LANES):
        ls = slice(t * LANES, (t + 1) * LANES)
        x_scr[t] = x[:, ls]
        for lo in range(4):
            x_scr4[t, lo] = x_scr[t, pl.ds(lo, QUARTER // 4, stride=4), :]
        for lo in range(4):
            for hi in range(4):
                res = lo + 4 * hi
                xp_scr[res * n:(res + 1) * n, ls] = _bf(x_scr4[t, lo, pl.ds(hi, n, stride=4), :])
    xp = xp_scr[...]
    for g in range(N_GROUPS):
        proj = jnp.dot(xp, w_ref[:, g * ATT_WIDTH:(g + 1) * ATT_WIDTH], preferred_element_type=jnp.float32)
        proj = proj * (ATT_HEAD ** -0.5 * LOG2E)
        if g == 0:
            for c in range(QUARTER // L):
                for res in range(0, 16, 2):
                    two = _bf(jnp.concatenate([proj[res * n + c * 8:res * n + (c + 1) * 8, :],
                                               proj[(res + 1) * n + c * 8:(res + 1) * n + (c + 1) * 8, :]], axis=0))
                    for hd in range(KV_HEADS):
                        q0_out[0, 0, hd, c, res * 8:(res + 2) * 8, :] = two[:, hd * ATT_HEAD:(hd + 1) * ATT_HEAD]
        elif g == 1:
            for lo in range(4):
                for hi in range(4):
                    res = lo + 4 * hi
                    piece = _bf(proj[res * n:(res + 1) * n, :])
                    for hd in range(KV_HEADS):
                        q1_out[0, 0, hd, 0, lo, hi * n:(hi + 1) * n, :] = piece[:, hd * ATT_HEAD:(hd + 1) * ATT_HEAD]
        else:
            for res in range(16):
                piece = _bf(proj[res * n:(res + 1) * n, :])
                for hd in range(KV_HEADS):
                    q2_out[0, 0, hd, res, :, :] = piece[:, hd * ATT_HEAD:(hd + 1) * ATT_HEAD]
    gate_out[0] = jnp.dot(_bf(x), w_ref[:, Q_WIDTH:], preferred_element_type=jnp.float32).astype(gate_out.dtype)


def _qproj(h, ln_g, w_in):
    B, S, D = h.shape
    rows = QUARTER
    n_tiles = S // ATT_ROWS
    tile = pl.BlockSpec((1, rows, D), lambda b, s: (b, s, 0))
    q_shape = jax.ShapeDtypeStruct((B, n_tiles, KV_HEADS, 16, BLOCK, ATT_HEAD), jnp.bfloat16)
    q0, q1, q2, gate = pl.pallas_call(
        _qproj_body,
        grid=(B, S // rows),
        in_specs=[tile, _const_spec((1, D)), _const_spec(w_in.shape)],
        out_specs=[pl.BlockSpec((1, 1, KV_HEADS, 4, BLOCK, ATT_HEAD), lambda b, s: (b, s // 4, 0, s % 4, 0, 0)),
                   pl.BlockSpec((1, 1, KV_HEADS, 1, 4, BLOCK, ATT_HEAD), lambda b, s: (b, s // 4, 0, s % 4, 0, 0, 0)),
                   pl.BlockSpec((1, 1, KV_HEADS, 16, BLOCK // 4, ATT_HEAD), lambda b, s: (b, s // 4, 0, 0, s % 4, 0)),
                   tile],
        out_shape=[q_shape,
                   jax.ShapeDtypeStruct((B, n_tiles, KV_HEADS, 4, 4, BLOCK, ATT_HEAD), jnp.bfloat16),
                   q_shape,
                   jax.ShapeDtypeStruct((B, S, ATT_WIDTH), ACT_DTYPE)],
        scratch_shapes=[pltpu.VMEM((D // LANES, rows, LANES), jnp.float32),
                        pltpu.VMEM((D // LANES, 4, rows // 4, LANES), jnp.float32),
                        pltpu.VMEM((rows, D), jnp.bfloat16)],
        compiler_params=_params(56, "parallel", "arbitrary"),
        name="att_qproj",
    )(h, ln_g.reshape(1, -1), _bf(w_in))
    return (q0, q1.reshape(q0.shape), q2), gate


def _att_body(q0_ref, q1_ref, q2_ref, k1_ref, v1_ref, k4_ref, v4_ref, k16_ref, v16_ref, out_ref,
              acc_ref, l_ref, m_ref):
    L = BLOCK
    head = pl.program_id(1)
    tile = pl.program_id(2)
    n_units = q0_ref.shape[0]
    slope = jnp.exp2(-(head + 1).astype(jnp.float32)) * LOG2E
    rho = lax.broadcasted_iota(jnp.int32, (L, 2 * L), 0)
    kj = lax.broadcasted_iota(jnp.int32, (L, 2 * L), 1)
    ones = jnp.ones((2 * L, ATT_HEAD), jnp.bfloat16)

    def bias_pair(pos):
        delta = L + pos - kj
        regular = jnp.where((delta >= 0) & (delta <= L), -slope * delta.astype(jnp.float32), NEG_BIG)
        delta = pos - kj
        first = jnp.where(delta >= 0, -slope * delta.astype(jnp.float32), NEG_BIG)
        return regular, first

    def batch_steps(q_ref, us, unit_io, regular, first):
        io = [unit_io(u) for u in us]
        s = [lax.dot_general(q_ref[u], k, _NT, preferred_element_type=jnp.float32)
             + (regular if is_first is None else jnp.where(is_first, first, regular))
             for u, (k, _, is_first, _) in zip(us, io)]
        yield
        m = [jnp.max(s_, axis=-1, keepdims=True) for s_ in s]
        p = [_bf(jnp.exp2(s_ - m_)) for s_, m_ in zip(s, m)]
        yield
        pv = [lax.dot_general(p_, jnp.concatenate([v, ones], axis=1), _NN, preferred_element_type=jnp.float32)
              for p_, (_, v, _, _) in zip(p, io)]
        yield
        for pv_, m_, (_, _, _, store) in zip(pv, m, io):
            store(acc_ref, pv_[:, :ATT_HEAD])
            store(l_ref, pv_[:, ATT_HEAD:])
            store(m_ref, jnp.broadcast_to(m_, (L, ATT_HEAD)))

    def start_of(block):
        return pl.multiple_of(jnp.maximum(block - 1, 0) * L, L)

    def io_1(u):
        block = tile * n_units + u
        rows = pl.ds(start_of(block), 2 * L)

        def store(ref, val):
            for res in range(16):
                ref[0, res, u * 8:(u + 1) * 8, :] = val[res * 8:(res + 1) * 8, :]
        return k1_ref[rows, :], v1_ref[rows, :], (block == 0 if u == 0 else None), store

    def io_4(u):
        c, res4 = u // 4, u % 4
        block = tile * 4 + c
        rows = pl.ds(start_of(block), 2 * L)

        def store(ref, val):
            for j in range(4):
                ref[1, 4 * j + res4, c * 32:(c + 1) * 32, :] = val[j * 32:(j + 1) * 32, :]
        return k4_ref[res4, rows, :], v4_ref[res4, rows, :], (block == 0 if c == 0 else None), store

    def io_16(u):
        rows = pl.ds(start_of(tile), 2 * L)

        def store(ref, val):
            ref[2, u] = val
        return k16_ref[u, rows, :], v16_ref[u, rows, :], tile == 0, store

    groups = ((q0_ref, (rho % 8) * 16 + rho // 8, io_1),
              (q1_ref, (rho % 32) * 4 + rho // 32, io_4),
              (q2_ref, rho, io_16))
    batches = []
    for q_ref, pos, unit_io in groups:
        regular, first = bias_pair(pos)
        for i in range(n_units // ATT_UNROLL):
            us = range(i * ATT_UNROLL, (i + 1) * ATT_UNROLL)
            batches.append(batch_steps(q_ref, us, unit_io, regular, first))
    live = list(range(len(batches)))
    tick = 0
    while live:
        for i in list(live):
            if tick >= i:
                try:
                    next(batches[i])
                except StopIteration:
                    live.remove(i)
        tick += 1

    m_all = jnp.maximum(jnp.maximum(m_ref[0], m_ref[1]), m_ref[2])
    num = jnp.zeros(m_all.shape, jnp.float32)
    den = jnp.zeros(m_all.shape, jnp.float32)
    for g in range(N_GROUPS):
        e = jnp.exp2(m_ref[g] - m_all)
        num = num + e * acc_ref[g]
        den = den + e * l_ref[g]
    merged = num / den
    for res in range(16):
        out_ref[0, pl.ds(res, L, stride=16), :] = merged[res]


def _attention(q, kv):
    B, _, S, _ = kv[0].shape
    rows = ATT_ROWS
    q_spec = pl.BlockSpec((None, None, None, 16, BLOCK, ATT_HEAD), lambda b, h, t: (b, t, h, 0, 0, 0))
    kv_specs = [pl.BlockSpec((None, None, S, ATT_HEAD), lambda b, h, t: (b, h, 0, 0))] * 2
    for d in DILATIONS[1:]:
        kv_specs += [pl.BlockSpec((None, None, d, S // d, ATT_HEAD), lambda b, h, t: (b, h, 0, 0, 0))] * 2
    scratch = pltpu.VMEM((N_GROUPS, 16, BLOCK, ATT_HEAD), jnp.float32)
    return pl.pallas_call(
        _att_body,
        grid=(B, KV_HEADS, S // rows),
        in_specs=[q_spec] * N_GROUPS + kv_specs,
        out_specs=pl.BlockSpec((1, rows, ATT_HEAD), lambda b, h, t: (b, t, h)),
        out_shape=jax.ShapeDtypeStruct((B, S, ATT_WIDTH), jnp.float32),
        scratch_shapes=[scratch, scratch, scratch],
        compiler_params=_params(48, "parallel", "parallel", "arbitrary"),
        name="dilated_attention",
    )(*q, *kv)


def kernel(x, p, a_ln_g, a_mu, a_w_rkvg, a_w0, a_w1, a_w2, a_a0, a_a1, a_a2, a_v0, a_v1, a_v2, a_k_k, a_k_a, a_r_k, a_gn_g, a_gn_b, a_w_o, kv_ln_g, w_kv, b_ln_g, b_w_in, b_w_o, ple_w, ple_gate_ln_g, ple_w_gate, final_ln_g):
    n_a = a_ln_g.shape[0]
    depth = p.shape[0]
    assert x.shape[1] % ATT_ROWS == 0 and x.shape[2] == D_MODEL
    h = x
    v_first = None
    kv_att = None
    for i in range(depth):
        if i < n_a:
            vres = None if i == 0 else (a_v0[i - 1], a_v1[i - 1], a_v2[i - 1])
            r, wl, k, v, a, gate = _rwkv_proj(h, v_first, a_ln_g[i], a_mu[i], a_w_rkvg[i], a_w0[i], a_w1[i], a_w2[i],
                                              a_a0[i], a_a1[i], a_a2[i], vres)
            if i == 0:
                v_first = v
            z = _wkv(r, wl, k, v, a, a_k_k[i], a_k_a[i], a_r_k[i], a_gn_g[i], a_gn_b[i])
            w_o = a_w_o[i]
        else:
            j = i - n_a
            q, gate = _qproj(h, b_ln_g[j], b_w_in[j])
            z = _attention(q, kv_att)
            w_o = b_w_o[j]
        if i == n_a - 1:
            h, *kv_att = _out_layer(z, gate, h, p, i, w_o, ple_w[i], ple_gate_ln_g[i], ple_w_gate[i],
                                    "kv", (kv_ln_g, w_kv))
        elif i == depth - 1:
            h = _out_layer(z, gate, h, p, i, w_o, ple_w[i], ple_gate_ln_g[i], ple_w_gate[i], "final", (final_ln_g,))
        else:
            h = _out_layer(z, gate, h, p, i, w_o, ple_w[i], ple_gate_ln_g[i], ple_w_gate[i])
    return h
```
